```python
import math
import jax
import jax.numpy as jnp
from jax import lax
import numpy as np

D_MODEL = 1024
BATCH = 16
SEQ = 4096
DEPTH = 1
DEC_BATCH = 128
DEC_SEQ = 4
PAST_LEN = 8192
PAGE_SIZE = 128

A_HEADS = 8
A_DK = 128
A_DV = 128
A_CHUNK = 32
B_HEADS = 8
B_KV_HEADS = 2
B_HEAD_DIM = 128
B_GROUP = B_HEADS // B_KV_HEADS
IDX_HEADS = 4
IDX_DIM = 64
TOPK_MAX = 256
Q_BLOCK = 64
N_BUCKETS = 32
MAX_DISTANCE = 128
PEER_HEADS = 8
N_KEYS = 128
N_EXPERTS = N_KEYS * N_KEYS
PEER_DKEY = 256
PEER_HALF = PEER_DKEY // 2
PEER_TOPK = 16
PEER_BLOCK = 256
ALPHA = (2.0 * DEPTH) ** 0.25
BETA = (8.0 * DEPTH) ** -0.25
LN_EPS = 1e-5
RMS_EPS = 1e-6
IN_SIZES = (A_HEADS * A_DK, A_HEADS * A_DK, A_HEADS * A_DV, A_HEADS * A_DV,
            B_HEADS * B_HEAD_DIM, B_KV_HEADS * B_HEAD_DIM, B_KV_HEADS * B_HEAD_DIM,
            IDX_HEADS * IDX_DIM, IDX_DIM, IDX_HEADS, D_MODEL, D_MODEL)
IN_WIDTH = sum(IN_SIZES)

kernel_name = 'hybrid_hgrn2_dsa_peer_step'

F32 = jnp.float32


def layer_norm(x, g, b):
    xf = x.astype(F32)
    mu = xf.mean(-1, keepdims=True)
    var = jnp.square(xf - mu).mean(-1, keepdims=True)
    return ((xf - mu) * lax.rsqrt(var + LN_EPS) * g.astype(F32) + b.astype(F32)).astype(x.dtype)


def t5_bucket(rel):
    max_exact = N_BUCKETS // 2
    relf = jnp.maximum(rel, 1).astype(F32)
    large = max_exact + (jnp.log(relf / max_exact) / math.log(MAX_DISTANCE / max_exact)
                         * (N_BUCKETS - max_exact)).astype(jnp.int32)
    large = jnp.minimum(large, N_BUCKETS - 1)
    return jnp.where(rel < max_exact, rel, large)


def take_rows(a, idx):
    return jax.vmap(lambda ab, ib: ab[ib])(a, idx)


def hgrn2_recurrence(q, logf, v, s0):
    B, T = q.shape[:2]
    chunk = math.gcd(T, A_CHUNK)
    nc = T // chunk

    def to_chunks(a):
        return a.reshape(B, nc, chunk, A_HEADS, a.shape[-1]).transpose(1, 0, 3, 2, 4)

    qc, fc, vc = to_chunks(q), to_chunks(logf), to_chunks(v)
    mask = jnp.tril(jnp.ones((chunk, chunk), dtype=bool))

    def step(S, inp):
        qb, lf, vb = inp
        qb, lf, vb = qb.astype(F32), lf.astype(F32), vb.astype(F32)
        bcum = jnp.cumsum(lf, axis=2)
        blast = bcum[:, :, -1:, :]
        k = -jnp.expm1(lf)
        qe = qb * jnp.exp(bcum)
        ke = k * jnp.exp(-bcum)
        kd = k * jnp.exp(blast - bcum)
        scores = jnp.where(mask, jnp.einsum('bhtk,bhsk->bhts', qe, ke), 0.0)
        o = (jnp.einsum('bhtk,bhkv->bhtv', qe, S)
             + jnp.einsum('bhts,bhsv->bhtv', scores, vb))
        S_new = jnp.exp(blast)[:, :, 0, :, None] * S + jnp.einsum('bhsk,bhsv->bhkv', kd, vb)
        return S_new, o

    S, o = lax.scan(step, s0.astype(F32), (qc, fc, vc))
    o = o.transpose(1, 0, 3, 2, 4).reshape(B, T, A_HEADS, A_DV)
    return o, S.astype(s0.dtype)


def sparse_attention(q, qi, wi, q_pos, kidx_all, gather_kv, t5_table):
    B, T = q.shape[:2]
    L = kidx_all.shape[1]
    k_sel = min(TOPK_MAX, L // 4)
    blk = math.gcd(T, Q_BLOCK)
    nb = T // blk
    key_pos = jnp.arange(L, dtype=jnp.int32)
    kif = kidx_all.astype(F32)

    def to_blocks(a):
        return a.reshape((B, nb, blk) + a.shape[2:]).swapaxes(0, 1)

    def one_block(args):
        qb, qib, wib, pb = args
        s = jax.nn.relu(jnp.einsum('bthd,bsd->bths', qib.astype(F32), kif) * IDX_DIM ** -0.5)
        score = jnp.einsum('bths,bth->bts', s, wib.astype(F32)) * IDX_HEADS ** -0.5
        score = jnp.where((key_pos[None, :] <= pb[:, None])[None], score, -jnp.inf)
        vals, idx = lax.top_k(score, k_sel)
        valid = jnp.isfinite(vals)
        kg, vg = gather_kv(idx)
        qg = qb.reshape(B, blk, B_KV_HEADS, B_GROUP, B_HEAD_DIM).astype(F32)
        logits = jnp.einsum('btgrd,btkgd->btgrk', qg, kg.astype(F32)) * B_HEAD_DIM ** -0.5
        rel = jnp.maximum(pb[None, :, None] - idx, 0)
        bias = t5_table[t5_bucket(rel)].astype(F32)
        bias = bias.reshape(B, blk, k_sel, B_KV_HEADS, B_GROUP).transpose(0, 1, 3, 4, 2)
        logits = jnp.where(valid[:, :, None, None, :], logits + bias, -jnp.inf)
        p = jax.nn.softmax(logits, axis=-1)
        o = jnp.einsum('btgrk,btkgd->btgrd', p, vg.astype(F32))
        return o.reshape(B, blk, B_HEADS * B_HEAD_DIM).astype(q.dtype)

    out = lax.map(one_block, (to_blocks(q), to_blocks(qi), to_blocks(wi), q_pos.reshape(nb, blk)))
    return out.swapaxes(0, 1).reshape(B, T, B_HEADS * B_HEAD_DIM)


def peer_ffn(h, w_q, sub_keys, u_tab, v_tab):
    B, T, D = h.shape
    n = B * T
    nb = -(-n // PEER_BLOCK)
    hp = jnp.pad(h.reshape(n, D), ((0, nb * PEER_BLOCK - n), (0, 0))).reshape(nb, PEER_BLOCK, D)

    def one_block(xb):
        q = (xb @ w_q).reshape(PEER_BLOCK, PEER_HEADS, 2, PEER_HALF).astype(F32)
        s = jnp.einsum('nhcd,ckd->nhck', q, sub_keys.astype(F32))
        s1, i1 = lax.top_k(s[:, :, 0], PEER_TOPK)
        s2, i2 = lax.top_k(s[:, :, 1], PEER_TOPK)
        cand = (s1[..., :, None] + s2[..., None, :]).reshape(PEER_BLOCK, PEER_HEADS, PEER_TOPK * PEER_TOPK)
        cid = (i1[..., :, None] * N_KEYS + i2[..., None, :]).reshape(PEER_BLOCK, PEER_HEADS, PEER_TOPK * PEER_TOPK)
        top, pos = lax.top_k(cand, PEER_TOPK)
        eid = jnp.take_along_axis(cid, pos, axis=-1)
        g = jax.nn.softmax(top, axis=-1)
        act = jax.nn.gelu(jnp.einsum('nhed,nd->nhe', u_tab[eid].astype(F32), xb.astype(F32)),
                          approximate=False)
        out = jnp.einsum('nhe,nhed->nd', g * act, v_tab[eid].astype(F32))
        return out.astype(h.dtype)

    return lax.map(one_block, hp).reshape(nb * PEER_BLOCK, D)[:n].reshape(B, T, D)


def decoder_layer(x, c, s0, q_pos, key_source, layer_idx, w_ada, b_ada, w_in, lb_param, a_gnorm,
                  t5_table, w_out, ln1_g, ln1_b, w_peer_q, peer_sub_keys, peer_u, peer_v,
                  ln2_g, ln2_b):
    B, T, _ = x.shape
    mod = jax.nn.silu(c) @ w_ada + b_ada
    sh1, sc1, g1, sh2, sc2, g2 = jnp.split(mod[:, None, :], 6, axis=-1)
    h = x * (1 + sc1) + sh1
    z = h @ w_in
    splits = [int(s) for s in np.cumsum(IN_SIZES)[:-1]]
    qa, fa, ia, ga, qb, kb, vb, qi, ki, wi, gate_a, gate_b = jnp.split(z, splits, axis=-1)

    lb = jnp.cumsum(jax.nn.softmax(lb_param.astype(F32), axis=0), axis=0)[layer_idx]
    logf = jnp.log(lb + (1.0 - lb) * jax.nn.sigmoid(fa.astype(F32)))
    oa, s_new = hgrn2_recurrence(jax.nn.silu(qa).reshape(B, T, A_HEADS, A_DK),
                                 logf.reshape(B, T, A_HEADS, A_DK),
                                 ia.reshape(B, T, A_HEADS, A_DV), s0)
    oa = oa * lax.rsqrt(jnp.mean(jnp.square(oa), axis=-1, keepdims=True) + RMS_EPS) * a_gnorm.astype(F32)
    oa = (oa.reshape(B, T, A_HEADS * A_DV) * jax.nn.silu(ga.astype(F32))).astype(x.dtype)

    kb = kb.reshape(B, T, B_KV_HEADS, B_HEAD_DIM)
    vb = vb.reshape(B, T, B_KV_HEADS, B_HEAD_DIM)
    kidx_all, gather_kv = key_source(kb, vb, ki)
    ob = sparse_attention(qb.reshape(B, T, B_HEADS, B_HEAD_DIM), qi.reshape(B, T, IDX_HEADS, IDX_DIM),
                          wi, q_pos, kidx_all, gather_kv, t5_table)

    merged = jax.nn.sigmoid(gate_a) * oa + jax.nn.sigmoid(gate_b) * ob
    x = layer_norm(ALPHA * x + g1 * (merged @ w_out), ln1_g, ln1_b)
    h2 = x * (1 + sc2) + sh2
    x = layer_norm(ALPHA * x + g2 * peer_ffn(h2, w_peer_q, peer_sub_keys, peer_u, peer_v), ln2_g, ln2_b)
    return x, kb, vb, ki, s_new


def setup_inputs(seed: int = 0) -> dict:
    key = jax.random.key(seed)
    ks = jax.random.split(key, 24)
    n_pages = PAST_LEN // PAGE_SIZE
    n_used = DEC_BATCH * n_pages
    n_pool = n_used + max(1, n_used // 4)

    def nrm(k, shape, s):
        return jax.random.normal(k, shape, F32) * s

    return {
        'x_prompt': nrm(ks[0], (BATCH, SEQ, D_MODEL), 1.0),
        'x_sample': nrm(ks[1], (DEC_BATCH, DEC_SEQ, D_MODEL), 1.0),
        'cache_k': nrm(ks[2], (DEPTH, n_pool, PAGE_SIZE, B_KV_HEADS, B_HEAD_DIM), 1.0),
        'cache_v': nrm(ks[3], (DEPTH, n_pool, PAGE_SIZE, B_KV_HEADS, B_HEAD_DIM), 1.0),
        'cache_kidx': nrm(ks[4], (DEPTH, n_pool, PAGE_SIZE, IDX_DIM), 1.0),
        'state_hgrn': nrm(ks[5], (DEPTH, DEC_BATCH, A_HEADS, A_DK, A_DV), 0.5),
        'page_table': jax.random.permutation(ks[6], n_pool)[:n_used].reshape(DEC_BATCH, n_pages).astype(jnp.int32),
        'c_prompt': nrm(ks[7], (BATCH, D_MODEL), 1.0),
        'c_sample': nrm(ks[8], (DEC_BATCH, D_MODEL), 1.0),
        'w_ada': nrm(ks[9], (DEPTH, D_MODEL, 6 * D_MODEL), 0.5 * D_MODEL ** -0.5),
        'b_ada': nrm(ks[10], (DEPTH, 6 * D_MODEL), 0.02),
        'w_in': nrm(ks[11], (DEPTH, D_MODEL, IN_WIDTH), D_MODEL ** -0.5),
        'lb_param': nrm(ks[12], (DEPTH + 1, A_HEADS * A_DK), 0.1),
        'a_gnorm': 1.0 + nrm(ks[13], (DEPTH, A_DV), 0.01),
        't5_table': nrm(ks[14], (N_BUCKETS, B_HEADS), 0.5),
        'w_out': nrm(ks[15], (DEPTH, D_MODEL, D_MODEL), BETA * D_MODEL ** -0.5),
        'ln1_g': 1.0 + nrm(ks[16], (DEPTH, D_MODEL), 0.01),
        'ln1_b': nrm(ks[17], (DEPTH, D_MODEL), 0.01),
        'w_peer_q': nrm(ks[18], (DEPTH, D_MODEL, PEER_HEADS * PEER_DKEY), D_MODEL ** -0.5),
        'peer_sub_keys': nrm(ks[19], (DEPTH, 2, N_KEYS, PEER_HALF), PEER_HALF ** -0.5),
        'peer_u': nrm(ks[20], (DEPTH, N_EXPERTS, D_MODEL), D_MODEL ** -0.5),
        'peer_v': nrm(ks[21], (DEPTH, N_EXPERTS, D_MODEL), BETA * PEER_HEADS ** -0.5),
        'ln2_g': 1.0 + nrm(ks[22], (DEPTH, D_MODEL), 0.01),
        'ln2_b': nrm(ks[23], (DEPTH, D_MODEL), 0.01),
    }


def reference(x_prompt, x_sample, cache_k, cache_v, cache_kidx, state_hgrn, page_table, c_prompt,
              c_sample, w_ada, b_ada, w_in, lb_param, a_gnorm, t5_table, w_out, ln1_g, ln1_b,
              w_peer_q, peer_sub_keys, peer_u, peer_v, ln2_g, ln2_b):
    past_len = page_table.shape[1] * PAGE_SIZE
    n_dec = page_table.shape[0]
    yp, ys = x_prompt, x_sample
    kp_l, vp_l, ip_l, sp_l, ks_l, vs_l, is_l, ss_l = [], [], [], [], [], [], [], []
    for l in range(DEPTH):
        wts = (w_ada[l], b_ada[l], w_in[l], lb_param, a_gnorm[l], t5_table, w_out[l], ln1_g[l],
               ln1_b[l], w_peer_q[l], peer_sub_keys[l], peer_u[l], peer_v[l], ln2_g[l], ln2_b[l])

        def prompt_keys(kb, vb, ki):
            return ki, (lambda idx: (take_rows(kb, idx), take_rows(vb, idx)))

        s0 = jnp.zeros((yp.shape[0], A_HEADS, A_DK, A_DV), x_prompt.dtype)
        yp, kp, vp, ip, sp = decoder_layer(yp, c_prompt, s0, jnp.arange(yp.shape[1], dtype=jnp.int32),
                                           prompt_keys, l, *wts)

        ck, cv, ci = cache_k[l], cache_v[l], cache_kidx[l]

        def sample_keys(kb, vb, ki):
            past_idx = ci[page_table].reshape(n_dec, past_len, IDX_DIM)
            kidx_all = jnp.concatenate([past_idx, ki], axis=1)

            def gather(idx):
                in_past = (idx < past_len)[..., None, None]
                pidx = jnp.minimum(idx, past_len - 1)
                page = jnp.take_along_axis(page_table, (pidx // PAGE_SIZE).reshape(n_dec, -1),
                                           axis=1).reshape(idx.shape)
                off = pidx % PAGE_SIZE
                nidx = jnp.clip(idx - past_len, 0, kb.shape[1] - 1)
                kg = jnp.where(in_past, ck[page, off], take_rows(kb, nidx))
                vg = jnp.where(in_past, cv[page, off], take_rows(vb, nidx))
                return kg, vg

            return kidx_all, gather

        ys, ksm, vsm, ism, ssm = decoder_layer(ys, c_sample, state_hgrn[l],
                                               past_len + jnp.arange(ys.shape[1], dtype=jnp.int32),
                                               sample_keys, l, *wts)
        kp_l.append(kp); vp_l.append(vp); ip_l.append(ip); sp_l.append(sp)
        ks_l.append(ksm); vs_l.append(vsm); is_l.append(ism); ss_l.append(ssm)

    k_prompt = jnp.stack(kp_l)
    v_prompt = jnp.stack(vp_l)
    kidx_prompt = jnp.stack(ip_l)
    s_prompt = jnp.stack(sp_l)
    k_sample = jnp.stack(ks_l)
    v_sample = jnp.stack(vs_l)
    kidx_sample = jnp.stack(is_l)
    s_sample = jnp.stack(ss_l)
    return (yp, ys, k_prompt, v_prompt, kidx_prompt, s_prompt, k_sample, v_sample, kidx_sample, s_sample)
```

```python
import functools
import math

import numpy as np
import jax
import jax.numpy as jnp
from jax import lax
from jax.experimental import pallas as pl
from jax.experimental.pallas import tpu as pltpu

F32 = jnp.float32
BF16 = jnp.bfloat16
I32 = jnp.int32

A_HEADS = 8
A_DK = 128
A_CHUNK = 32
B_HEADS = 8
B_KV_HEADS = 2
B_GROUP = 4
B_HEAD_DIM = 128
IDX_HEADS = 4
IDX_DIM = 64
TOPK_MAX = 256
N_BUCKETS = 32
MAX_DISTANCE = 128
PEER_HEADS = 8
N_KEYS = 128
PEER_TOPK = 16
PAGE_SIZE = 128
LN_EPS = 1e-5
RMS_EPS = 1e-6

Z_A = 0
Z_QB = 4096
Z_KB = 5120
Z_VB = 5376
Z_QI = 5632
Z_KI = 5888
Z_WI = 5952
Z_GATES = 6144
Z_WIDTH = 8192

NEG = -1e30
INT_MIN = -2 ** 31
NT_DIMS = (((1,), (1,)), ((), ()))
VMEM_LIMIT = 56 * 1024 * 1024


def _cparams(sem):
    return pltpu.CompilerParams(dimension_semantics=sem, vmem_limit_bytes=VMEM_LIMIT)


def _sigmoid(x):
    return jax.nn.sigmoid(x)


def _ada_body(c_ref, w_ref, b_ref, o_ref):
    c = c_ref[...]
    s = c * _sigmoid(c)
    o_ref[...] = jnp.dot(s.astype(BF16), w_ref[...], preferred_element_type=F32) + b_ref[...]


def _ada(c, w_bf, b):
    R, D = c.shape
    N = w_bf.shape[1]
    tn = N // 4
    return pl.pallas_call(
        _ada_body, grid=(N // tn,),
        in_specs=[pl.BlockSpec((R, D), lambda j: (0, 0)),
                  pl.BlockSpec((D, tn), lambda j: (0, j)),
                  pl.BlockSpec((1, tn), lambda j: (0, j))],
        out_specs=pl.BlockSpec((R, tn), lambda j: (0, j)),
        out_shape=jax.ShapeDtypeStruct((R, N), F32),
        compiler_params=_cparams(("parallel",)), name="ada")(c, w_bf, b)


def _mod_spec(mod, blk, tm, rows_per_mod, D, nidx):
    if mod.ndim == 3:
        if nidx == 1:
            return pl.BlockSpec((None, 1, D), lambda i: ((i * tm) // rows_per_mod, 0, blk))
        return pl.BlockSpec((None, 1, D), lambda i, j: ((i * tm) // rows_per_mod, 0, blk))
    if nidx == 1:
        return pl.BlockSpec((tm, D), lambda i: (i, blk))
    return pl.BlockSpec((tm, D), lambda i, j: (i, blk))


def _inproj_body(x_ref, sh_ref, sc_ref, w_ref, o_ref):
    h = x_ref[...] * (1.0 + sc_ref[...]) + sh_ref[...]
    o_ref[...] = jnp.dot(h.astype(BF16), w_ref[...], preferred_element_type=F32)


def _inproj(x2d, mod, rows_per_mod, w_bf, tm, tn):
    N, D = x2d.shape
    W = w_bf.shape[1]
    return pl.pallas_call(
        _inproj_body, grid=(N // tm, W // tn),
        in_specs=[pl.BlockSpec((tm, D), lambda i, j: (i, 0)),
                  _mod_spec(mod, 0, tm, rows_per_mod, D, 2),
                  _mod_spec(mod, 1, tm, rows_per_mod, D, 2),
                  pl.BlockSpec((D, tn), lambda i, j: (0, j))],
        out_specs=pl.BlockSpec((tm, tn), lambda i, j: (i, j)),
        out_shape=jax.ShapeDtypeStruct((N, W), F32),
        compiler_params=_cparams(("parallel", "arbitrary")), name="inproj")(x2d, mod, mod, w_bf)


def _hgrn_body(q_ref, f_ref, i_ref, g_ref, lbp_ref, gn_ref, s0_ref, o_ref, sout_ref, st_scr,
               *, C, Tb, t_valid):
    tb = pl.program_id(2)

    @pl.when(tb == 0)
    def _init():
        st_scr[...] = s0_ref[...].T

    lbp = lbp_ref[...]
    mx = jnp.max(lbp, axis=0, keepdims=True)
    e = jnp.exp(lbp - mx)
    lb = e[0:1, :] / jnp.sum(e, axis=0, keepdims=True)

    f = lb + (1.0 - lb) * _sigmoid(f_ref[...])
    lf = jnp.log(f)
    k = 1.0 - f
    if t_valid < Tb:
        ok = lax.broadcasted_iota(I32, (Tb, A_DK), 0) < t_valid
        lf = jnp.where(ok, lf, 0.0)
        k = jnp.where(ok, k, 0.0)

    shift = int(math.log2(C))
    r = lax.broadcasted_iota(I32, (Tb, Tb), 0)
    c = lax.broadcasted_iota(I32, (Tb, Tb), 1)
    tril = (jnp.right_shift(r, shift) == jnp.right_shift(c, shift)) & (c <= r)
    bcum = jnp.dot(jnp.where(tril, 1.0, 0.0), lf, precision=lax.Precision.HIGHEST,
                   preferred_element_type=F32)
    nchunk = Tb // C
    tot = jnp.concatenate(
        [jnp.broadcast_to(bcum[(j + 1) * C - 1:(j + 1) * C, :], (C, A_DK)) for j in range(nchunk)], axis=0)

    qa = q_ref[...]
    qe = (qa * _sigmoid(qa)) * jnp.exp(bcum)
    ke = k * jnp.exp(-bcum)
    kd = k * jnp.exp(tot - bcum)
    v = i_ref[...]
    qe_b = qe.astype(BF16)
    scores = lax.dot_general(qe_b, ke.astype(BF16), NT_DIMS, preferred_element_type=F32)
    scores = jnp.where(tril, scores, 0.0)
    o = jnp.dot(scores.astype(BF16), v.astype(BF16), preferred_element_type=F32)

    if Tb < 128:
        zpad = jnp.zeros((128 - Tb, A_DK), F32)
        v_p = jnp.concatenate([v, zpad], axis=0)
        kd_p = jnp.concatenate([kd, zpad], axis=0)
    else:
        v_p, kd_p = v, kd
    Tp = v_p.shape[0]
    vT = v_p.T
    kd_b = kd_p.astype(BF16)
    colchunk = jnp.right_shift(lax.broadcasted_iota(I32, (A_DK, Tp), 1), shift)

    ST = st_scr[...]
    o_inter = []
    for j in range(nchunk):
        o_inter.append(lax.dot_general(qe_b[j * C:(j + 1) * C], ST.astype(BF16), NT_DIMS,
                                       preferred_element_type=F32))
        d_c = jnp.exp(tot[j * C:j * C + 1, :])
        vTm = jnp.where(colchunk == j, vT, 0.0) if nchunk > 1 else vT
        ST = ST * d_c + jnp.dot(vTm.astype(BF16), kd_b, preferred_element_type=F32)
    st_scr[...] = ST
    o = o + jnp.concatenate(o_inter, axis=0)

    ms = jnp.mean(o * o, axis=-1, keepdims=True)
    ga = g_ref[...]
    o_ref[...] = o * lax.rsqrt(ms + RMS_EPS) * gn_ref[...] * (ga * _sigmoid(ga))

    @pl.when(tb == pl.num_programs(2) - 1)
    def _fin():
        sout_ref[...] = ST.T


def _hgrn(z, s0, lb_param, gnorm, B, T, Tb, C, t_valid):
    nt = T // Tb
    H = A_HEADS

    def col(base):
        return pl.BlockSpec((Tb, A_DK), lambda b, h, t: (b * nt + t, base + h))

    return pl.pallas_call(
        functools.partial(_hgrn_body, C=C, Tb=Tb, t_valid=t_valid),
        grid=(B, H, nt),
        in_specs=[col(0), col(8), col(16), col(24),
                  pl.BlockSpec((2, A_DK), lambda b, h, t: (0, h)),
                  pl.BlockSpec((1, A_DK), lambda b, h, t: (0, 0)),
                  pl.BlockSpec((None, None, A_DK, A_DK), lambda b, h, t: (b, h, 0, 0))],
        out_specs=[pl.BlockSpec((Tb, A_DK), lambda b, h, t: (b * nt + t, h)),
                   pl.BlockSpec((None, None, A_DK, A_DK), lambda b, h, t: (b, h, 0, 0))],
        out_shape=[jax.ShapeDtypeStruct((B * T, H * A_DK), F32),
                   jax.ShapeDtypeStruct((B, H, A_DK, A_DK), F32)],
        scratch_shapes=[pltpu.VMEM((A_DK, A_DK), F32)],
        compiler_params=_cparams(("parallel", "parallel", "arbitrary")), name="hgrn",
    )(z, z, z, z, lb_param, gnorm, s0)


def _t5_bucket_np(rel):
    rel = np.asarray(rel, np.int64)
    max_exact = N_BUCKETS // 2
    relf = np.maximum(rel, 1).astype(np.float64)
    large = max_exact + (np.log(relf / max_exact) / math.log(MAX_DISTANCE / max_exact)
                         * (N_BUCKETS - max_exact)).astype(np.int64)
    large = np.minimum(large, N_BUCKETS - 1)
    return np.where(rel < max_exact, rel, large).astype(np.int32)


def _bias_body(tab_ref, bkt_ref, o_ref):
    h = pl.program_id(0)
    b = bkt_ref[...]
    acc = jnp.zeros(b.shape, F32)
    for i in range(N_BUCKETS):
        acc = jnp.where(b == i, tab_ref[i, h], acc)
    o_ref[...] = acc


def _bias_tiles(t5_table, buckets):
    K, R, Cc = buckets.shape
    return pl.pallas_call(
        _bias_body, grid=(B_HEADS, K),
        in_specs=[pl.BlockSpec(memory_space=pltpu.SMEM),
                  pl.BlockSpec((None, R, Cc), lambda h, k: (k, 0, 0))],
        out_specs=pl.BlockSpec((None, None, R, Cc), lambda h, k: (h, k, 0, 0)),
        out_shape=jax.ShapeDtypeStruct((B_HEADS, K, R, Cc), F32),
        compiler_params=_cparams(("parallel", "parallel")), name="t5_bias")(t5_table, jnp.asarray(buckets))


def _ordered_key(score):
    score = jnp.where(score == 0.0, 0.0, score)
    bits = pltpu.bitcast(score, I32)
    return jnp.where(bits < 0, jnp.bitwise_xor(bits, jnp.int32(0x7FFFFFFF)), bits)


def _bcast_lanes(col, n):
    return jnp.broadcast_to(col, (col.shape[0], n))


def _dsa_prompt_body(q_ref, kv_ref, idxq_ref, ki_ref, bias_ref, o_ref, key_scr, mb_scr, *, tq, ksel):
    qi = pl.program_id(1)
    nvalid = qi + 1
    nslab = tq // 128

    idxq = idxq_ref[...]
    q_idx = [idxq[:, h * IDX_DIM:(h + 1) * IDX_DIM].astype(BF16) for h in range(IDX_HEADS)]
    wi = idxq[:, Z_WI - Z_QI:Z_WI - Z_QI + IDX_HEADS]
    w_cols = [wi[:, h:h + 1] for h in range(IDX_HEADS)]
    row_t = qi * tq + lax.broadcasted_iota(I32, (tq, tq), 0)
    col_i = lax.broadcasted_iota(I32, (tq, tq), 1)

    def idx_chunk(c, carry):
        off = pl.multiple_of(c * tq, tq)
        kic = ki_ref[pl.ds(off, tq), 0:IDX_DIM].astype(BF16)
        acc = jnp.zeros((tq, tq), F32)
        for h in range(IDX_HEADS):
            s = lax.dot_general(q_idx[h], kic, NT_DIMS, preferred_element_type=F32)
            acc = acc + jnp.maximum(s * (IDX_DIM ** -0.5), 0.0) * w_cols[h]
        score = acc * (IDX_HEADS ** -0.5)
        score = jnp.where(off + col_i <= row_t, score, -jnp.inf)
        key_scr[:, pl.ds(off, tq)] = _ordered_key(score)
        return carry

    lax.fori_loop(0, nvalid, idx_chunk, 0)

    def count(pred_fn):
        def body(c, cnt):
            off = pl.multiple_of(c * tq, tq)
            x = key_scr[:, pl.ds(off, tq)]
            for s in range(nslab):
                cnt = cnt + jnp.where(pred_fn(x[:, s * 128:(s + 1) * 128]), 1.0, 0.0)
            return cnt
        cnt = lax.fori_loop(0, nvalid, body, jnp.zeros((tq, 128), F32))
        return jnp.sum(cnt, axis=1, keepdims=True)

    def bit_step(it, thr):
        cand = thr + jnp.left_shift(jnp.int32(1), 31 - it)
        tot = count(lambda x: x >= cand)
        return jnp.where(_bcast_lanes(tot, 128) >= ksel, cand, thr)

    thr = lax.fori_loop(0, 32, bit_step, jnp.full((tq, 128), INT_MIN, I32))
    need = ksel - count(lambda x: x > thr)
    thr_w = jnp.concatenate([thr] * nslab, axis=1)
    need_w = _bcast_lanes(need, tq)
    triu = jnp.where(lax.broadcasted_iota(I32, (tq, tq), 0) < col_i, 1.0, 0.0).astype(BF16)

    def mask_chunk(c, carry):
        off = pl.multiple_of(c * tq, tq)
        x = key_scr[:, pl.ds(off, tq)]
        eq = x == thr_w
        eqf = jnp.where(eq, 1.0, 0.0)
        rank = jnp.dot(eqf.astype(BF16), triu, preferred_element_type=F32) + carry
        mb = jnp.where(x > thr_w, 0.0, jnp.where(eq, jnp.where(rank < need_w, 0.0, NEG), NEG))
        mb = jnp.where(off + col_i <= row_t, mb, NEG)
        mb_scr[:, pl.ds(off, tq)] = mb
        return carry + jnp.sum(eqf, axis=1, keepdims=True)

    lax.fori_loop(0, nvalid, mask_chunk, jnp.zeros((tq, 1), F32))

    scale = B_HEAD_DIM ** -0.5
    for h in range(B_HEADS):
        g = h // B_GROUP
        qh = (q_ref[:, h * B_HEAD_DIM:(h + 1) * B_HEAD_DIM] * scale).astype(BF16)

        def att_chunk(c, carry, h=h, g=g, qh=qh):
            m, l, acc = carry
            off = pl.multiple_of(c * tq, tq)
            kc = kv_ref[pl.ds(off, tq), g * B_HEAD_DIM:(g + 1) * B_HEAD_DIM].astype(BF16)
            vc = kv_ref[pl.ds(off, tq), (B_KV_HEADS + g) * B_HEAD_DIM:(B_KV_HEADS + g + 1) * B_HEAD_DIM].astype(BF16)
            s = lax.dot_general(qh, kc, NT_DIMS, preferred_element_type=F32)
            s = s + mb_scr[:, pl.ds(off, tq)] + bias_ref[h, jnp.minimum(qi - c, 2)]
            m_new = jnp.maximum(m, jnp.max(s, axis=1, keepdims=True))
            p = jnp.exp(s - m_new)
            alpha = jnp.exp(m - m_new)
            l = alpha * l + jnp.sum(p, axis=1, keepdims=True)
            acc = alpha * acc + jnp.dot(p.astype(BF16), vc, preferred_element_type=F32)
            return m_new, l, acc

        m, l, acc = lax.fori_loop(
            0, nvalid, att_chunk,
            (jnp.full((tq, 1), NEG, F32), jnp.zeros((tq, 1), F32), jnp.zeros((tq, B_HEAD_DIM), F32)))
        o_ref[:, h * B_HEAD_DIM:(h + 1) * B_HEAD_DIM] = acc * (1.0 / l)


def _dsa_prompt(z, bias, B, T, tq):
    nq = T // tq
    ksel = min(TOPK_MAX, T // 4)
    assert tq >= ksel and tq % 128 == 0
    return pl.pallas_call(
        functools.partial(_dsa_prompt_body, tq=tq, ksel=ksel),
        grid=(B, nq),
        in_specs=[pl.BlockSpec((tq, 1024), lambda b, i: (b * nq + i, Z_QB // 1024)),
                  pl.BlockSpec((T, 512), lambda b, i: (b, Z_KB // 512)),
                  pl.BlockSpec((tq, 512), lambda b, i: (b * nq + i, Z_QI // 512)),
                  pl.BlockSpec((T, 128), lambda b, i: (b, Z_KI // 128)),
                  pl.BlockSpec(bias.shape, lambda b, i: (0, 0, 0, 0))],
        out_specs=pl.BlockSpec((tq, 1024), lambda b, i: (b * nq + i, 0)),
        out_shape=jax.ShapeDtypeStruct((B * T, 1024), F32),
        scratch_shapes=[pltpu.VMEM((tq, T), I32), pltpu.VMEM((tq, T), F32)],
        compiler_params=_cparams(("parallel", "arbitrary")), name="dsa_prompt",
    )(z, z, z, z, bias)


PG = 8
QR = 8


def _dsa_s_select_body(pt_ref, qi_ref, w_ref, kin_ref, *rest, n_pages, ksel, t_new):
    pages = rest[:PG]
    mb_ref = rest[PG]
    key_scr = rest[PG + 1]
    j = pl.program_id(1)
    L = n_pages * PAGE_SIZE
    nslab = n_pages + 1
    qi32 = qi_ref[...]
    w32 = w_ref[...]

    def chunk_score(keys_bf):
        s = lax.dot_general(qi32, keys_bf, NT_DIMS, preferred_element_type=F32)
        s = jnp.maximum(s * (IDX_DIM ** -0.5), 0.0) * w32
        acc = s[0:QR]
        for h in range(1, IDX_HEADS):
            acc = acc + s[h * QR:(h + 1) * QR]
        return acc * (IDX_HEADS ** -0.5)

    for p in range(PG):
        off = pl.multiple_of((j * PG + p) * PAGE_SIZE, PAGE_SIZE)
        key_scr[:, pl.ds(off, PAGE_SIZE)] = _ordered_key(chunk_score(pages[p][...].astype(BF16)))

    @pl.when(j == pl.num_programs(1) - 1)
    def _select():
        row = lax.broadcasted_iota(I32, (QR, PAGE_SIZE), 0)
        col = lax.broadcasted_iota(I32, (QR, PAGE_SIZE), 1)
        new_ok = (col <= row) & (col < t_new)
        s_new = jnp.where(new_ok, chunk_score(kin_ref[...]), -jnp.inf)
        key_scr[:, L:L + PAGE_SIZE] = _ordered_key(s_new)

        def count(pred_fn):
            cnt = jnp.zeros((QR, 128), F32)
            for s in range(nslab):
                cnt = cnt + jnp.where(pred_fn(key_scr[:, s * 128:(s + 1) * 128]), 1.0, 0.0)
            return jnp.sum(cnt, axis=1, keepdims=True)

        def bit_step(it, thr):
            cand = thr + jnp.left_shift(jnp.int32(1), 31 - it)
            tot = count(lambda x: x >= cand)
            return jnp.where(_bcast_lanes(tot, 128) >= ksel, cand, thr)

        thr = lax.fori_loop(0, 32, bit_step, jnp.full((QR, 128), INT_MIN, I32))
        need = _bcast_lanes(ksel - count(lambda x: x > thr), 128)
        triu = jnp.where(lax.broadcasted_iota(I32, (128, 128), 0) < lax.broadcasted_iota(I32, (128, 128), 1),
                         1.0, 0.0).astype(BF16)
        ones = jnp.ones((128, 128), BF16)
        carry = jnp.zeros((QR, 128), F32)
        for s in range(nslab):
            x = key_scr[:, s * 128:(s + 1) * 128]
            eq = x == thr
            eqb = jnp.where(eq, 1.0, 0.0).astype(BF16)
            rank = jnp.dot(eqb, triu, preferred_element_type=F32) + carry
            mb = jnp.where(x > thr, 0.0, jnp.where(eq, jnp.where(rank < need, 0.0, NEG), NEG))
            if s == nslab - 1:
                mb = jnp.where(new_ok, mb, NEG)
            mb_ref[:, s * 128:(s + 1) * 128] = mb
            carry = carry + jnp.dot(eqb, ones, preferred_element_type=F32)


def _dsa_s_select(page_table, qi32, w32, ki_new, cache_kidx, ksel, t_new):
    nb, n_pages = page_table.shape
    nj = n_pages // PG
    Lp = (n_pages + 1) * PAGE_SIZE

    def page_spec(p):
        return pl.BlockSpec((None, PAGE_SIZE, IDX_DIM), lambda b, j, pt: (pt[b, j * PG + p], 0, 0))

    grid_spec = pltpu.PrefetchScalarGridSpec(
        num_scalar_prefetch=1, grid=(nb, nj),
        in_specs=[pl.BlockSpec((None, IDX_HEADS * QR, IDX_DIM), lambda b, j, pt: (b, 0, 0)),
                  pl.BlockSpec((None, IDX_HEADS * QR, 128), lambda b, j, pt: (b, 0, 0)),
                  pl.BlockSpec((None, PAGE_SIZE, IDX_DIM), lambda b, j, pt: (b, 0, 0))]
                 + [page_spec(p) for p in range(PG)],
        out_specs=pl.BlockSpec((None, QR, Lp), lambda b, j, pt: (b, 0, 0)),
        scratch_shapes=[pltpu.VMEM((QR, Lp), I32)])
    return pl.pallas_call(
        functools.partial(_dsa_s_select_body, n_pages=n_pages, ksel=ksel, t_new=t_new),
        grid_spec=grid_spec,
        out_shape=jax.ShapeDtypeStruct((nb, QR, Lp), F32),
        compiler_params=_cparams(("parallel", "arbitrary")), name="dsa_sample_select",
    )(page_table, qi32, w32, ki_new, *([cache_kidx] * PG))


def _dsa_s_attn_body(pt_ref, tab_ref, q_ref, mb_ref, mbn_ref, kvn_ref, bias_ref, *rest):
    kpages = rest[:PG]
    vpages = rest[PG:2 * PG]
    o_ref = rest[2 * PG]
    m_scr, l_scr, acc_scr = rest[2 * PG + 1:]
    j = pl.program_id(1)
    last = j == pl.num_programs(1) - 1
    R = B_GROUP * QR
    scale = B_HEAD_DIM ** -0.5

    @pl.when(j == 0)
    def _init():
        m_scr[...] = jnp.full(m_scr.shape, NEG, F32)
        l_scr[...] = jnp.zeros(l_scr.shape, F32)
        acc_scr[...] = jnp.zeros(acc_scr.shape, F32)

    def far_bias(g):
        return jnp.concatenate(
            [jnp.full((QR, PAGE_SIZE), tab_ref[N_BUCKETS - 1, g * B_GROUP + r], F32) for r in range(B_GROUP)], axis=0)

    def near_bias(g, kind):
        return jnp.concatenate([bias_ref[g * B_GROUP + r, kind] for r in range(B_GROUP)], axis=0)

    def update(g, s, vs):
        m_old = m_scr[g]
        m_new = m_old
        for t in s:
            m_new = jnp.maximum(m_new, jnp.max(t, axis=1, keepdims=True))
        alpha = jnp.exp(m_old - m_new)
        l = alpha * l_scr[g]
        acc = alpha * acc_scr[g]
        for t, vt in zip(s, vs):
            p = jnp.exp(t - m_new)
            l = l + jnp.sum(p, axis=1, keepdims=True)
            acc = acc + jnp.dot(p.astype(BF16), vt, preferred_element_type=F32)
        m_scr[g] = m_new
        l_scr[g] = l
        acc_scr[g] = acc

    for g in range(B_KV_HEADS):
        qg = (q_ref[g] * scale).astype(BF16)
        lo, hi = g * B_HEAD_DIM, (g + 1) * B_HEAD_DIM
        fb = far_bias(g)
        tiles, vals = [], []
        for p in range(PG):
            kp = kpages[p][:, lo:hi].astype(BF16)
            s = lax.dot_general(qg, kp, NT_DIMS, preferred_element_type=F32)
            mb = mb_ref[:, p * PAGE_SIZE:(p + 1) * PAGE_SIZE]
            s = s + jnp.concatenate([mb] * B_GROUP, axis=0)
            if p == PG - 1:
                s = s + jnp.where(last, near_bias(g, 0), fb)
            else:
                s = s + fb
            tiles.append(s)
            vals.append(vpages[p][:, lo:hi].astype(BF16))
        update(g, tiles, vals)

    @pl.when(last)
    def _fin():
        for g in range(B_KV_HEADS):
            qg = (q_ref[g] * scale).astype(BF16)
            lo, hi = g * B_HEAD_DIM, (g + 1) * B_HEAD_DIM
            kn = kvn_ref[:, lo:hi].astype(BF16)
            vn = kvn_ref[:, B_KV_HEADS * B_HEAD_DIM + lo:B_KV_HEADS * B_HEAD_DIM + hi].astype(BF16)
            s = lax.dot_general(qg, kn, NT_DIMS, preferred_element_type=F32)
            s = s + jnp.concatenate([mbn_ref[...]] * B_GROUP, axis=0) + near_bias(g, 1)
            update(g, [s], [vn])
            o_ref[g] = acc_scr[g] * (1.0 / l_scr[g])


def _dsa_s_attn(page_table, t5_table, q32, mb, kv_new, bias_s, cache_k, cache_v):
    nb, n_pages = page_table.shape
    nj = n_pages // PG
    R = B_GROUP * QR
    CW = B_KV_HEADS * B_HEAD_DIM

    def page_spec(p):
        return pl.BlockSpec((None, PAGE_SIZE, CW), lambda b, j, pt: (pt[b, j * PG + p], 0, 0))

    grid_spec = pltpu.PrefetchScalarGridSpec(
        num_scalar_prefetch=1, grid=(nb, nj),
        in_specs=[pl.BlockSpec(memory_space=pltpu.SMEM),
                  pl.BlockSpec((None, B_KV_HEADS, R, B_HEAD_DIM), lambda b, j, pt: (b, 0, 0, 0)),
                  pl.BlockSpec((None, QR, PG * PAGE_SIZE), lambda b, j, pt: (b, 0, j)),
                  pl.BlockSpec((None, QR, PAGE_SIZE), lambda b, j, pt: (b, 0, n_pages)),
                  pl.BlockSpec((None, PAGE_SIZE, 2 * CW), lambda b, j, pt: (b, 0, 0)),
                  pl.BlockSpec(bias_s.shape, lambda b, j, pt: (0, 0, 0, 0))]
                 + [page_spec(p) for p in range(PG)] + [page_spec(p) for p in range(PG)],
        out_specs=pl.BlockSpec((None, B_KV_HEADS, R, B_HEAD_DIM), lambda b, j, pt: (b, 0, 0, 0)),
        scratch_shapes=[pltpu.VMEM((B_KV_HEADS, R, 1), F32), pltpu.VMEM((B_KV_HEADS, R, 1), F32),
                        pltpu.VMEM((B_KV_HEADS, R, B_HEAD_DIM), F32)])
    return pl.pallas_call(
        _dsa_s_attn_body, grid_spec=grid_spec,
        out_shape=jax.ShapeDtypeStruct((nb, B_KV_HEADS, R, B_HEAD_DIM), F32),
        compiler_params=_cparams(("parallel", "arbitrary")), name="dsa_sample_attn",
    )(page_table, t5_table, q32, mb, mb, kv_new, bias_s, *([cache_k] * PG), *([cache_v] * PG))


def _layer_norm(r, g, b):
    mu = jnp.mean(r, axis=-1, keepdims=True)
    d = r - mu
    var = jnp.mean(d * d, axis=-1, keepdims=True)
    return d * lax.rsqrt(var + LN_EPS) * g + b


def _merge_body(oa_ref, ob_ref, gates_ref, x_ref, g1_ref, sh2_ref, sc2_ref, w_ref, lng_ref, lnb_ref,
                x1_ref, h2_ref, *, alpha):
    D = oa_ref.shape[1]
    merged = _sigmoid(gates_ref[:, 0:D]) * oa_ref[...] + _sigmoid(gates_ref[:, D:2 * D]) * ob_ref[...]
    y = jnp.dot(merged.astype(BF16), w_ref[...], preferred_element_type=F32)
    x1 = _layer_norm(alpha * x_ref[...] + g1_ref[...] * y, lng_ref[...], lnb_ref[...])
    x1_ref[...] = x1
    h2_ref[...] = (x1 * (1.0 + sc2_ref[...]) + sh2_ref[...]).astype(BF16)


def _merge(oa, ob, z, x2d, mod, rows_per_mod, w_out_bf, ln_g, ln_b, alpha, tm):
    N, D = x2d.shape
    row = lambda i: (i, 0)
    const = lambda i: (0, 0)
    return pl.pallas_call(
        functools.partial(_merge_body, alpha=alpha), grid=(N // tm,),
        in_specs=[pl.BlockSpec((tm, D), row), pl.BlockSpec((tm, D), row),
                  pl.BlockSpec((tm, 2 * D), lambda i: (i, Z_GATES // (2 * D))),
                  pl.BlockSpec((tm, D), row),
                  _mod_spec(mod, 2, tm, rows_per_mod, D, 1),
                  _mod_spec(mod, 3, tm, rows_per_mod, D, 1),
                  _mod_spec(mod, 4, tm, rows_per_mod, D, 1),
                  pl.BlockSpec((D, D), const), pl.BlockSpec((1, D), const), pl.BlockSpec((1, D), const)],
        out_specs=[pl.BlockSpec((tm, D), row), pl.BlockSpec((tm, D), row)],
        out_shape=[jax.ShapeDtypeStruct((N, D), F32), jax.ShapeDtypeStruct((N, D), BF16)],
        compiler_params=_cparams(("parallel",)), name="merge_out",
    )(oa, ob, z, x2d, mod, mod, mod, w_out_bf, ln_g, ln_b)


def _top_values(x, n):
    out = []
    for r in range(n):
        m = jnp.max(x, axis=0, keepdims=True)
        out.append(m)
        if r < n - 1:
            x = jnp.where(x == m, -jnp.inf, x)
    return out


_PEER_PAIRS = [(a, b) for a in range(PEER_TOPK) for b in range(PEER_TOPK) if (a + 1) * (b + 1) <= PEER_TOPK]


def _peer_body(h2_ref, x1_ref, g2_ref, wq_ref, sk_ref, u_ref, vt_ref, lng_ref, lnb_ref, o_ref,
               s1_scr, s2_scr, e1_scr, e2_scr, thr_scr, acc_scr, *, tn, ec, alpha):
    e = pl.program_id(1)
    n_i1 = ec // N_KEYS

    @pl.when(e == 0)
    def _select():
        q = jnp.dot(h2_ref[...], wq_ref[...], preferred_element_type=F32)
        for h in range(PEER_HEADS):
            tops = []
            for c in range(2):
                qhc = q[:, (2 * h + c) * N_KEYS:(2 * h + c + 1) * N_KEYS].astype(BF16)
                sT = lax.dot_general(sk_ref[c], qhc, NT_DIMS, preferred_element_type=F32)
                (s1_scr if c == 0 else s2_scr)[h] = sT
                tops.append(_top_values(sT, PEER_TOPK))
            t1, t2 = tops
            cand = jnp.concatenate([t1[a] + t2[b] for a, b in _PEER_PAIRS], axis=0)
            thr = _top_values(cand, PEER_TOPK)[-1]
            mx = t1[0] + t2[0]
            zsum = jnp.sum(jnp.where(cand >= thr, jnp.exp(cand - mx), 0.0), axis=0, keepdims=True)
            thr_scr[h:h + 1, :] = thr
            e1_scr[h] = jnp.exp(s1_scr[h] - t1[0]) * (1.0 / zsum)
            e2_scr[h] = jnp.exp(s2_scr[h] - t2[0])

    act = lax.dot_general(u_ref[...], h2_ref[...], NT_DIMS, preferred_element_type=F32)
    rows = []
    for jj in range(n_i1):
        i1 = e * n_i1 + jj
        a = act[jj * N_KEYS:(jj + 1) * N_KEYS, :]
        gsum = jnp.zeros((N_KEYS, tn), F32)
        for h in range(PEER_HEADS):
            ssum = s2_scr[h] + s1_scr[h, pl.ds(i1, 1), :]
            gsum = gsum + jnp.where(ssum >= thr_scr[h:h + 1, :], e2_scr[h], 0.0) * e1_scr[h, pl.ds(i1, 1), :]
        gelu = 0.5 * a * (1.0 + lax.erf(a * (2.0 ** -0.5)))
        rows.append((gsum * gelu).astype(BF16))
    w = jnp.concatenate(rows, axis=0)
    contrib = jnp.dot(vt_ref[...], w, preferred_element_type=F32)

    @pl.when(e == 0)
    def _first():
        acc_scr[...] = contrib

    @pl.when(e > 0)
    def _rest():
        acc_scr[...] += contrib

    @pl.when(e == pl.num_programs(1) - 1)
    def _fin():
        y = acc_scr[...].T
        o_ref[...] = _layer_norm(alpha * x1_ref[...] + g2_ref[...] * y, lng_ref[...], lnb_ref[...])


def _peer(h2, x1, mod, rows_per_mod, wq_bf, sk_bf, u_bf, vt_bf, ln_g, ln_b, alpha, tn, ec):
    N, D = x1.shape
    E = u_bf.shape[0]
    row = lambda i, e: (i, 0)
    const2 = lambda i, e: (0, 0)
    if mod.ndim == 3:
        g2_spec = pl.BlockSpec((None, 1, D), lambda i, e: ((i * tn) // rows_per_mod, 0, 5))
    else:
        g2_spec = pl.BlockSpec((tn, D), lambda i, e: (i, 5))
    return pl.pallas_call(
        functools.partial(_peer_body, tn=tn, ec=ec, alpha=alpha), grid=(N // tn, E // ec),
        in_specs=[pl.BlockSpec((tn, D), row), pl.BlockSpec((tn, D), row), g2_spec,
                  pl.BlockSpec(wq_bf.shape, const2),
                  pl.BlockSpec(sk_bf.shape, lambda i, e: (0, 0, 0)),
                  pl.BlockSpec((ec, D), lambda i, e: (e, 0)),
                  pl.BlockSpec((D, ec), lambda i, e: (0, e)),
                  pl.BlockSpec((1, D), const2), pl.BlockSpec((1, D), const2)],
        out_specs=pl.BlockSpec((tn, D), row),
        out_shape=jax.ShapeDtypeStruct((N, D), F32),
        scratch_shapes=[pltpu.VMEM((PEER_HEADS, N_KEYS, tn), F32), pltpu.VMEM((PEER_HEADS, N_KEYS, tn), F32),
                        pltpu.VMEM((PEER_HEADS, N_KEYS, tn), F32), pltpu.VMEM((PEER_HEADS, N_KEYS, tn), F32),
                        pltpu.VMEM((PEER_HEADS, tn), F32), pltpu.VMEM((D, tn), F32)],
        compiler_params=_cparams(("parallel", "arbitrary")), name="peer",
    )(h2, x1, mod, wq_bf, sk_bf, u_bf, vt_bf, ln_g, ln_b)


def _pack_w_in(w_in):
    D = w_in.shape[0]
    idx_w = w_in[:, 5632:5956]
    pad = jnp.zeros((D, Z_GATES - Z_QI - idx_w.shape[1]), w_in.dtype)
    return jnp.concatenate([w_in[:, :5632], idx_w, pad, w_in[:, 5956:]], axis=1).astype(BF16)


def _prompt_buckets(tq):
    i = np.arange(tq)[:, None]
    j = np.arange(tq)[None, :]
    far = np.full((tq, tq), N_BUCKETS - 1, np.int32)
    return np.stack([_t5_bucket_np(np.maximum(i - j, 0)), _t5_bucket_np(tq + i - j), far])


def _sample_buckets():
    t = np.arange(QR)[:, None]
    o = np.arange(PAGE_SIZE)[None, :]
    return np.stack([_t5_bucket_np(PAGE_SIZE + t - o), _t5_bucket_np(np.maximum(t - o, 0))])


def _layer_tail(oa, ob, z, x2d, mod, rows_per_mod, wts, alpha, tm, tn_peer):
    x1, h2 = _merge(oa, ob, z, x2d, mod, rows_per_mod, wts["w_out"], wts["ln1_g"], wts["ln1_b"], alpha, tm)
    return _peer(h2, x1, mod, rows_per_mod, wts["w_peer_q"], wts["sub_keys"], wts["peer_u"], wts["peer_vt"],
                 wts["ln2_g"], wts["ln2_b"], alpha, tn_peer, 1024)


def kernel(x_prompt, x_sample, cache_k, cache_v, cache_kidx, state_hgrn, page_table, c_prompt, c_sample,
           w_ada, b_ada, w_in, lb_param, a_gnorm, t5_table, w_out, ln1_g, ln1_b, w_peer_q, peer_sub_keys,
           peer_u, peer_v, ln2_g, ln2_b):
    depth = w_ada.shape[0]
    assert depth == 1 and lb_param.shape[0] == 2
    alpha = (2.0 * depth) ** 0.25
    B, T, D = x_prompt.shape
    NB, TS, _ = x_sample.shape
    n_pages = page_table.shape[1]
    past_len = n_pages * PAGE_SIZE

    wts = dict(
        w_out=w_out[0].astype(BF16), ln1_g=ln1_g, ln1_b=ln1_b, ln2_g=ln2_g, ln2_b=ln2_b,
        w_peer_q=w_peer_q[0].astype(BF16), sub_keys=peer_sub_keys[0].astype(BF16),
        peer_u=peer_u[0].astype(BF16), peer_vt=peer_v[0].T.astype(BF16))
    w_in_bf = _pack_w_in(w_in[0])
    gnorm = a_gnorm.reshape(1, A_DK)

    mod = _ada(jnp.concatenate([c_prompt, c_sample], axis=0), w_ada[0].astype(BF16), b_ada)
    mod_p = mod[:B].reshape(B, 1, 6 * D)
    mod_s = jnp.repeat(mod[B:], TS, axis=0)

    xp = x_prompt.reshape(B * T, D)
    zp = _inproj(xp, mod_p, T, w_in_bf, 512, 1024)
    oa_p, s_p = _hgrn(zp, jnp.zeros((B, A_HEADS, A_DK, A_DK), F32), lb_param, gnorm, B, T, 256, A_CHUNK, 256)
    tq = 256
    bias_p = _bias_tiles(t5_table, _prompt_buckets(tq))
    ob_p = _dsa_prompt(zp, bias_p, B, T, tq)
    y_p = _layer_tail(oa_p, ob_p, zp, xp, mod_p, T, wts, alpha, 256, 512)

    xs = x_sample.reshape(NB * TS, D)
    zs = _inproj(xs, mod_s, 1, w_in_bf, NB * TS, 1024)
    zs3 = zs.reshape(NB, TS, Z_WIDTH)
    zs_pad = jnp.pad(zs3, ((0, 0), (0, QR - TS), (0, 0))).reshape(NB * QR, Z_WIDTH)
    assert TS <= QR
    oa_s8, s_s = _hgrn(zs_pad, state_hgrn[0], lb_param, gnorm, NB, QR, QR, QR, TS)
    oa_s = oa_s8.reshape(NB, QR, D)[:, :TS].reshape(NB * TS, D)

    qi_s = zs3[:, :, Z_QI:Z_QI + IDX_HEADS * IDX_DIM].reshape(NB, TS, IDX_HEADS, IDX_DIM)
    qi32 = jnp.pad(qi_s.transpose(0, 2, 1, 3), ((0, 0), (0, 0), (0, QR - TS), (0, 0)))
    qi32 = qi32.reshape(NB, IDX_HEADS * QR, IDX_DIM).astype(BF16)
    wi_s = zs3[:, :, Z_WI:Z_WI + IDX_HEADS].transpose(0, 2, 1)
    w32 = jnp.pad(wi_s, ((0, 0), (0, 0), (0, QR - TS))).reshape(NB, IDX_HEADS * QR, 1)
    w32 = jnp.broadcast_to(w32, (NB, IDX_HEADS * QR, 128))
    ki_new = jnp.pad(zs3[:, :, Z_KI:Z_KI + IDX_DIM], ((0, 0), (0, PAGE_SIZE - TS), (0, 0))).astype(BF16)
    kv_new = jnp.pad(zs3[:, :, Z_KB:Z_KB + 512], ((0, 0), (0, PAGE_SIZE - TS), (0, 0)))
    q_s = zs3[:, :, Z_QB:Z_QB + 1024].reshape(NB, TS, B_KV_HEADS, B_GROUP, B_HEAD_DIM)
    q32 = jnp.pad(q_s.transpose(0, 2, 3, 1, 4), ((0, 0), (0, 0), (0, 0), (0, QR - TS), (0, 0)))
    q32 = q32.reshape(NB, B_KV_HEADS, B_GROUP * QR, B_HEAD_DIM)

    ksel = min(TOPK_MAX, (past_len + TS) // 4)
    mb_s = _dsa_s_select(page_table, qi32, w32, ki_new, cache_kidx[0], ksel, TS)
    bias_s = _bias_tiles(t5_table, _sample_buckets())
    ck = cache_k[0].reshape(cache_k.shape[1], PAGE_SIZE, B_KV_HEADS * B_HEAD_DIM)
    cv = cache_v[0].reshape(cache_v.shape[1], PAGE_SIZE, B_KV_HEADS * B_HEAD_DIM)
    o32 = _dsa_s_attn(page_table, t5_table, q32, mb_s, kv_new, bias_s, ck, cv)
    ob_s = o32.reshape(NB, B_KV_HEADS, B_GROUP, QR, B_HEAD_DIM)[:, :, :, :TS]
    ob_s = ob_s.transpose(0, 3, 1, 2, 4).reshape(NB * TS, D)
    y_s = _layer_tail(oa_s, ob_s, zs, xs, mod_s, 1, wts, alpha, NB * TS, NB * TS)

    def kv_out(z, nb, t):
        k = z[:, Z_KB:Z_KB + 256].reshape(1, nb, t, B_KV_HEADS, B_HEAD_DIM)
        v = z[:, Z_VB:Z_VB + 256].reshape(1, nb, t, B_KV_HEADS, B_HEAD_DIM)
        ki = z[:, Z_KI:Z_KI + IDX_DIM].reshape(1, nb, t, IDX_DIM)
        return k, v, ki

    kp, vp, ip = kv_out(zp, B, T)
    ks, vs, is_ = kv_out(zs, NB, TS)
    return (y_p.reshape(B, T, D), y_s.reshape(NB, TS, D), kp, vp, ip, s_p[None],
            ks, vs, is_, s_s[None])
```

```python
import functools
import math

import numpy as np
import jax
import jax.numpy as jnp
from jax import lax
from jax.experimental import pallas as pl
from jax.experimental.pallas import tpu as pltpu

F32 = jnp.float32
BF16 = jnp.bfloat16
I32 = jnp.int32

A_HEADS = 8
A_DK = 128
A_CHUNK = 32
B_HEADS = 8
B_KV_HEADS = 2
B_GROUP = 4
B_HEAD_DIM = 128
IDX_HEADS = 4
IDX_DIM = 64
TOPK_MAX = 256
N_BUCKETS = 32
MAX_DISTANCE = 128
PEER_HEADS = 8
N_KEYS = 128
PEER_TOPK = 16
PAGE_SIZE = 128
LN_EPS = 1e-5
RMS_EPS = 1e-6

Z_A = 0
Z_QB = 4096
Z_KB = 5120
Z_VB = 5376
Z_QI = 5632
Z_KI = 5888
Z_WI = 5952
Z_GATES = 6144
Z_WIDTH = 8192

NEG = -1e30
INT_MIN = -2 ** 31
LOG2E = 1.4426950408889634
NT_DIMS = (((1,), (1,)), ((), ()))
VMEM_LIMIT = 56 * 1024 * 1024


def _cparams(sem):
    return pltpu.CompilerParams(dimension_semantics=sem, vmem_limit_bytes=VMEM_LIMIT)


def _resident(shape, index_map):
    return pl.BlockSpec(shape, index_map, pipeline_mode=pl.Buffered(1))


def _sigmoid(x):
    return jax.nn.sigmoid(x)


def _ada_body(c_ref, w_ref, b_ref, o_ref):
    c = c_ref[...]
    s = c * _sigmoid(c)
    o_ref[...] = jnp.dot(s.astype(BF16), w_ref[...], preferred_element_type=F32) + b_ref[...]


def _ada(c, w_bf, b):
    R, D = c.shape
    N = w_bf.shape[1]
    tn = N // 4
    return pl.pallas_call(
        _ada_body, grid=(N // tn,),
        in_specs=[pl.BlockSpec((R, D), lambda j: (0, 0)),
                  pl.BlockSpec((D, tn), lambda j: (0, j)),
                  pl.BlockSpec((1, tn), lambda j: (0, j))],
        out_specs=pl.BlockSpec((R, tn), lambda j: (0, j)),
        out_shape=jax.ShapeDtypeStruct((R, N), F32),
        compiler_params=_cparams(("parallel",)), name="ada")(c, w_bf, b)


def _mod_spec(mod, blk, tm, rows_per_mod, D, nidx):
    if mod.ndim == 3:
        if nidx == 1:
            return pl.BlockSpec((None, 1, D), lambda i: ((i * tm) // rows_per_mod, 0, blk))
        return pl.BlockSpec((None, 1, D), lambda i, j: ((i * tm) // rows_per_mod, 0, blk))
    if nidx == 1:
        return pl.BlockSpec((tm, D), lambda i: (i, blk))
    return pl.BlockSpec((tm, D), lambda i, j: (i, blk))


def _inproj_body(x_ref, sh_ref, sc_ref, w_ref, o_ref):
    h = x_ref[...] * (1.0 + sc_ref[...]) + sh_ref[...]
    o_ref[...] = jnp.dot(h.astype(BF16), w_ref[...], preferred_element_type=F32)


def _inproj(x2d, mod, rows_per_mod, w_bf, tm, tn):
    N, D = x2d.shape
    W = w_bf.shape[1]
    return pl.pallas_call(
        _inproj_body, grid=(N // tm, W // tn),
        in_specs=[pl.BlockSpec((tm, D), lambda i, j: (i, 0)),
                  _mod_spec(mod, 0, tm, rows_per_mod, D, 2),
                  _mod_spec(mod, 1, tm, rows_per_mod, D, 2),
                  pl.BlockSpec((D, tn), lambda i, j: (0, j))],
        out_specs=pl.BlockSpec((tm, tn), lambda i, j: (i, j)),
        out_shape=jax.ShapeDtypeStruct((N, W), F32),
        compiler_params=_cparams(("parallel", "arbitrary")), name="inproj")(x2d, mod, mod, w_bf)


def _hgrn_body(q_ref, f_ref, i_ref, g_ref, lbp_ref, gn_ref, s0_ref, o_ref, sout_ref, st_scr,
               *, C, Tb, t_valid, HB):
    tb = pl.program_id(2)
    shift = int(math.log2(C))
    nchunk = Tb // C
    r = lax.broadcasted_iota(I32, (Tb, Tb), 0)
    c = lax.broadcasted_iota(I32, (Tb, Tb), 1)
    tril = (jnp.right_shift(r, shift) == jnp.right_shift(c, shift)) & (c <= r)
    trilf = jnp.where(tril, 1.0, 0.0)
    Tp = max(Tb, 128)
    colchunk = jnp.right_shift(lax.broadcasted_iota(I32, (A_DK, Tp), 1), shift)

    lbp = lbp_ref[...]
    mx = jnp.max(lbp, axis=0, keepdims=True)
    e = jnp.exp(lbp - mx)
    lb_all = e[0:1, :] / jnp.sum(e, axis=0, keepdims=True)

    @pl.when(tb == 0)
    def _init():
        for hh in range(HB):
            st_scr[hh] = s0_ref[hh].T

    W = HB * A_DK
    f = lb_all + (1.0 - lb_all) * _sigmoid(f_ref[...])
    lf = jnp.log(f)
    k = 1.0 - f
    if t_valid < Tb:
        ok = lax.broadcasted_iota(I32, (Tb, W), 0) < t_valid
        lf = jnp.where(ok, lf, 0.0)
        k = jnp.where(ok, k, 0.0)
    bcum = jnp.dot(trilf, lf, precision=lax.Precision.HIGHEST,
                   preferred_element_type=F32)
    tot = jnp.concatenate(
        [jnp.broadcast_to(bcum[(j + 1) * C - 1:(j + 1) * C, :], (C, W)) for j in range(nchunk)], axis=0)
    qa = q_ref[...]
    qe_b = ((qa * _sigmoid(qa)) * jnp.exp(bcum)).astype(BF16)
    ke_b = (k * jnp.exp(-bcum)).astype(BF16)
    kd = k * jnp.exp(tot - bcum)
    v = i_ref[...]
    v_b = v.astype(BF16)
    if Tb < 128:
        zpad = jnp.zeros((128 - Tb, W), F32)
        v_p = jnp.concatenate([v, zpad], axis=0)
        kd_p = jnp.concatenate([kd, zpad], axis=0)
    else:
        v_p, kd_p = v, kd
    kd_b = kd_p.astype(BF16)
    heads = [(hh * A_DK, (hh + 1) * A_DK) for hh in range(HB)]

    scores = [jnp.where(tril, lax.dot_general(qe_b[:, lo:hi], ke_b[:, lo:hi], NT_DIMS,
                                              preferred_element_type=F32), 0.0).astype(BF16) for lo, hi in heads]
    o = [jnp.dot(scores[hh], v_b[:, lo:hi], preferred_element_type=F32) for hh, (lo, hi) in enumerate(heads)]
    vT = [v_p[:, lo:hi].T for lo, hi in heads]
    ST = [st_scr[hh] for hh in range(HB)]
    o_inter = [[] for _ in range(HB)]
    for j in range(nchunk):
        for hh, (lo, hi) in enumerate(heads):
            o_inter[hh].append(lax.dot_general(qe_b[j * C:(j + 1) * C, lo:hi], ST[hh].astype(BF16), NT_DIMS,
                                               preferred_element_type=F32))
            d_c = jnp.exp(tot[j * C:j * C + 1, lo:hi])
            vTm = jnp.where(colchunk == j, vT[hh], 0.0) if nchunk > 1 else vT[hh]
            ST[hh] = ST[hh] * d_c + jnp.dot(vTm.astype(BF16), kd_b[:, lo:hi], preferred_element_type=F32)
    gn = gn_ref[...]
    ga = g_ref[...]
    gate = ga * _sigmoid(ga)
    for hh, (lo, hi) in enumerate(heads):
        st_scr[hh] = ST[hh]
        oh = o[hh] + jnp.concatenate(o_inter[hh], axis=0)
        ms = jnp.mean(oh * oh, axis=-1, keepdims=True)
        o_ref[:, lo:hi] = oh * lax.rsqrt(ms + RMS_EPS) * gn * gate[:, lo:hi]

    @pl.when(tb == pl.num_programs(2) - 1)
    def _fin():
        for hh in range(HB):
            sout_ref[hh] = st_scr[hh].T


def _hgrn(z, s0, lb_param, gnorm, B, T, Tb, C, t_valid, HB):
    nt = T // Tb
    H = A_HEADS
    W = HB * A_DK
    ng = H // HB

    def col(group):
        return pl.BlockSpec((Tb, W), lambda b, h, t: (b * nt + t, group * ng + h))

    return pl.pallas_call(
        functools.partial(_hgrn_body, C=C, Tb=Tb, t_valid=t_valid, HB=HB),
        grid=(B, ng, nt),
        in_specs=[col(0), col(1), col(2), col(3),
                  pl.BlockSpec((2, W), lambda b, h, t: (0, h)),
                  pl.BlockSpec((1, A_DK), lambda b, h, t: (0, 0)),
                  pl.BlockSpec((None, HB, A_DK, A_DK), lambda b, h, t: (b, h, 0, 0))],
        out_specs=[pl.BlockSpec((Tb, W), lambda b, h, t: (b * nt + t, h)),
                   pl.BlockSpec((None, HB, A_DK, A_DK), lambda b, h, t: (b, h, 0, 0))],
        out_shape=[jax.ShapeDtypeStruct((B * T, H * A_DK), F32),
                   jax.ShapeDtypeStruct((B, H, A_DK, A_DK), F32)],
        scratch_shapes=[pltpu.VMEM((HB, A_DK, A_DK), F32)],
        compiler_params=_cparams(("parallel", "parallel", "arbitrary")), name="hgrn",
    )(z, z, z, z, lb_param, gnorm, s0)


def _t5_bucket_np(rel):
    rel = np.asarray(rel, np.int64)
    max_exact = N_BUCKETS // 2
    relf = np.maximum(rel, 1).astype(np.float64)
    large = max_exact + (np.log(relf / max_exact) / math.log(MAX_DISTANCE / max_exact)
                         * (N_BUCKETS - max_exact)).astype(np.int64)
    large = np.minimum(large, N_BUCKETS - 1)
    return np.where(rel < max_exact, rel, large).astype(np.int32)


def _bias_body(tab_ref, bkt_ref, o_ref):
    h = pl.program_id(0)
    b = bkt_ref[...]
    acc = jnp.zeros(b.shape, F32)
    for i in range(N_BUCKETS):
        acc = jnp.where(b == i, tab_ref[i, h], acc)
    o_ref[...] = (acc - tab_ref[N_BUCKETS - 1, h]) * LOG2E


def _bias_tiles(t5_table, buckets):
    K, R, Cc = buckets.shape
    return pl.pallas_call(
        _bias_body, grid=(B_HEADS, K),
        in_specs=[pl.BlockSpec(memory_space=pltpu.SMEM),
                  pl.BlockSpec((None, R, Cc), lambda h, k: (k, 0, 0))],
        out_specs=pl.BlockSpec((None, None, R, Cc), lambda h, k: (h, k, 0, 0)),
        out_shape=jax.ShapeDtypeStruct((B_HEADS, K, R, Cc), F32),
        compiler_params=_cparams(("parallel", "parallel")), name="t5_bias")(t5_table, jnp.asarray(buckets))


def _ordered_key(score):
    score = jnp.where(score == 0.0, 0.0, score)
    bits = pltpu.bitcast(score, I32)
    return jnp.where(bits < 0, jnp.bitwise_xor(bits, jnp.int32(0x7FFFFFFF)), bits)


def _bcast_lanes(col, n):
    return jnp.broadcast_to(col, (col.shape[0], n))


def _dsa_prompt_body(q_ref, kv_ref, idxq_ref, ki_ref, bias_ref, o_ref,
                     key_scr, mb_scr, m_scr, l_scr, acc_scr, *, tq, ksel):
    qi = pl.program_id(1)
    nvalid = qi + 1
    npair = (nvalid + 1) // 2
    tw = 2 * tq

    idxq = idxq_ref[...]
    q_idx = [idxq[:, h * IDX_DIM:(h + 1) * IDX_DIM].astype(BF16) for h in range(IDX_HEADS)]
    w_t = idxq[:, Z_KI - Z_QI:Z_KI - Z_QI + 128].T
    w_rows = [w_t[Z_WI - Z_KI + h:Z_WI - Z_KI + h + 1, :] for h in range(IDX_HEADS)]
    key_i = lax.broadcasted_iota(I32, (tq, tq), 0)
    qry_t = qi * tq + lax.broadcasted_iota(I32, (tq, tq), 1)

    def idx_chunk(c, carry):
        off = pl.multiple_of(c * tq, tq)
        kic = ki_ref[pl.ds(off, tq), 0:IDX_DIM].astype(BF16)
        acc = jnp.zeros((tq, tq), F32)
        for h in range(IDX_HEADS):
            s = lax.dot_general(kic, q_idx[h], NT_DIMS, preferred_element_type=F32)
            acc = acc + jnp.maximum(s * (IDX_DIM ** -0.5), 0.0) * w_rows[h]
        score = acc * (IDX_HEADS ** -0.5)
        score = jnp.where(off + key_i <= qry_t, score, -jnp.inf)
        key_scr[pl.ds(off, tq), :] = _ordered_key(score)
        return carry

    lax.fori_loop(0, nvalid, idx_chunk, 0)

    @pl.when(nvalid % 2 == 1)
    def _pad():
        off = pl.multiple_of(nvalid * tq, tq)
        key_scr[pl.ds(off, tq), :] = jnp.full((tq, tq), INT_MIN, I32)
        mb_scr[:, pl.ds(off, tq)] = jnp.full((tq, tq), NEG, F32)

    NACC = 8

    def count(pred_fn):
        def body(c, cnt):
            off = pl.multiple_of(c * tw, tw)
            x = key_scr[pl.ds(off, tw), :].reshape(tw // (8 * NACC), NACC, 8, tq)
            return cnt + jnp.sum(jnp.where(pred_fn(x), 1.0, 0.0), axis=0)
        cnt = lax.fori_loop(0, npair, body, jnp.zeros((NACC, 8, tq), F32))
        return jnp.sum(jnp.sum(cnt, axis=0), axis=0, keepdims=True)

    def bit_step(it, thr):
        cand = thr + jnp.left_shift(jnp.int32(1), 31 - it)
        return jnp.where(count(lambda x: x >= cand) >= ksel, cand, thr)

    thr = lax.fori_loop(0, 32, bit_step, jnp.full((1, tq), INT_MIN, I32))
    need = ksel - count(lambda x: x > thr)
    tril = jnp.where(lax.broadcasted_iota(I32, (tq, tq), 1) < key_i, 1.0, 0.0).astype(BF16)

    def mask_chunk(c, carry):
        off = pl.multiple_of(c * tq, tq)
        x = key_scr[pl.ds(off, tq), :]
        eq = x == thr
        eqf = jnp.where(eq, 1.0, 0.0)
        rank = jnp.dot(tril, eqf.astype(BF16), preferred_element_type=F32) + carry
        mb = jnp.where(x > thr, 0.0, jnp.where(eq, jnp.where(rank < need, 0.0, NEG), NEG))
        mb = jnp.where(off + key_i <= qry_t, mb, NEG)
        mb_scr[:, pl.ds(off, tq)] = mb.T
        return carry + jnp.sum(eqf, axis=0, keepdims=True)

    lax.fori_loop(0, nvalid, mask_chunk, jnp.zeros((1, tq), F32))

    scale = (B_HEAD_DIM ** -0.5) * LOG2E
    nsl = tw // 128

    def bias_kind(c):
        d = qi - c
        return jnp.where(d == 0, 0, jnp.where(d == 1, 1, 2))

    for g in range(B_KV_HEADS):
        qg = jnp.concatenate(
            [q_ref[:, (g * B_GROUP + r) * B_HEAD_DIM:(g * B_GROUP + r + 1) * B_HEAD_DIM] for r in range(B_GROUP)],
            axis=0)
        qg = (qg * scale).astype(BF16)

        def logits(c2, g=g, qg=qg):
            off = pl.multiple_of(c2 * tw, tw)
            kc = kv_ref[pl.ds(off, tw), g * B_HEAD_DIM:(g + 1) * B_HEAD_DIM].astype(BF16)
            s = lax.dot_general(qg, kc, NT_DIMS, preferred_element_type=F32)
            mb = mb_scr[:, pl.ds(off, tw)]
            bias = jnp.concatenate([bias_ref[g, bias_kind(2 * c2)], bias_ref[g, bias_kind(2 * c2 + 1)]], axis=1)
            return jnp.concatenate([s[r * tq:(r + 1) * tq] + mb for r in range(B_GROUP)], axis=0) + bias

        def max_body(c2, m):
            s = logits(c2)
            for k in range(nsl):
                m = jnp.maximum(m, s[:, k * 128:(k + 1) * 128])
            return m

        m_scr[...] = lax.fori_loop(0, npair, max_body, jnp.full(m_scr.shape, NEG, F32))
        m_row = _bcast_lanes(jnp.max(m_scr[...], axis=1, keepdims=True), 128)
        m_scr[...] = m_row
        l_scr[...] = jnp.zeros(l_scr.shape, F32)
        acc_scr[...] = jnp.zeros(acc_scr.shape, F32)

        def sum_body(c2, carry, g=g):
            off = pl.multiple_of(c2 * tw, tw)
            vc = kv_ref[pl.ds(off, tw), (B_KV_HEADS + g) * B_HEAD_DIM:(B_KV_HEADS + g + 1) * B_HEAD_DIM].astype(BF16)
            m = m_scr[...]
            p = jnp.exp2(logits(c2) - jnp.concatenate([m] * nsl, axis=1))
            l = l_scr[...]
            for k in range(nsl):
                l = l + p[:, k * 128:(k + 1) * 128]
            l_scr[...] = l
            acc_scr[...] += jnp.dot(p.astype(BF16), vc, preferred_element_type=F32)
            return carry

        lax.fori_loop(0, npair, sum_body, 0)
        out = acc_scr[...] * (1.0 / jnp.sum(l_scr[...], axis=1, keepdims=True))
        for r in range(B_GROUP):
            h = g * B_GROUP + r
            o_ref[:, h * B_HEAD_DIM:(h + 1) * B_HEAD_DIM] = out[r * tq:(r + 1) * tq]


def _dsa_prompt(z, bias_g, B, T, tq):
    nq = T // tq
    ksel = min(TOPK_MAX, T // 4)
    assert tq >= ksel and tq % 128 == 0 and nq % 2 == 0
    R = B_GROUP * tq
    return pl.pallas_call(
        functools.partial(_dsa_prompt_body, tq=tq, ksel=ksel),
        grid=(B, nq),
        in_specs=[pl.BlockSpec((tq, 1024), lambda b, i: (b * nq + i, Z_QB // 1024)),
                  _resident((T, 512), lambda b, i: (b, Z_KB // 512)),
                  pl.BlockSpec((tq, 512), lambda b, i: (b * nq + i, Z_QI // 512)),
                  _resident((T, 128), lambda b, i: (b, Z_KI // 128)),
                  _resident(bias_g.shape, lambda b, i: (0, 0, 0, 0))],
        out_specs=pl.BlockSpec((tq, 1024), lambda b, i: (b * nq + i, 0)),
        out_shape=jax.ShapeDtypeStruct((B * T, 1024), F32),
        scratch_shapes=[pltpu.VMEM((T, tq), I32), pltpu.VMEM((tq, T), F32),
                        pltpu.VMEM((R, 128), F32), pltpu.VMEM((R, 128), F32), pltpu.VMEM((R, B_HEAD_DIM), F32)],
        compiler_params=_cparams(("parallel", "arbitrary")), name="dsa_prompt",
    )(z, z, z, z, bias_g)


PG = 8
QR = 8


def _dsa_s_select_body(pt_ref, qi_ref, w_ref, kin_ref, *rest, n_pages, ksel, t_new):
    pages = rest[:PG]
    mb_ref = rest[PG]
    key_scr = rest[PG + 1]
    j = pl.program_id(1)
    L = n_pages * PAGE_SIZE
    nslab = n_pages + 1
    qi32 = qi_ref[...]
    w32 = w_ref[...]

    def chunk_score(keys_t_bf):
        s = jnp.dot(qi32, keys_t_bf, preferred_element_type=F32)
        s = jnp.maximum(s * (IDX_DIM ** -0.5), 0.0) * w32
        acc = s[0:QR]
        for h in range(1, IDX_HEADS):
            acc = acc + s[h * QR:(h + 1) * QR]
        return acc * (IDX_HEADS ** -0.5)

    for p in range(PG):
        off = pl.multiple_of((j * PG + p) * PAGE_SIZE, PAGE_SIZE)
        key_scr[:, pl.ds(off, PAGE_SIZE)] = _ordered_key(chunk_score(pages[p][...].astype(BF16)))

    @pl.when(j == pl.num_programs(1) - 1)
    def _select():
        row = lax.broadcasted_iota(I32, (QR, PAGE_SIZE), 0)
        col = lax.broadcasted_iota(I32, (QR, PAGE_SIZE), 1)
        new_ok = (col <= row) & (col < t_new)
        s_new = jnp.where(new_ok, chunk_score(kin_ref[...]), -jnp.inf)
        key_scr[:, L:L + PAGE_SIZE] = _ordered_key(s_new)

        def count(pred_fn):
            cnt = jnp.zeros((QR, 128), F32)
            for s in range(nslab):
                cnt = cnt + jnp.where(pred_fn(key_scr[:, s * 128:(s + 1) * 128]), 1.0, 0.0)
            return jnp.sum(cnt, axis=1, keepdims=True)

        def bit_step(it, thr):
            cand = thr + jnp.left_shift(jnp.int32(1), 31 - it)
            tot = count(lambda x: x >= cand)
            return jnp.where(_bcast_lanes(tot, 128) >= ksel, cand, thr)

        thr = lax.fori_loop(0, 32, bit_step, jnp.full((QR, 128), INT_MIN, I32))
        need = _bcast_lanes(ksel - count(lambda x: x > thr), 128)
        triu = jnp.where(lax.broadcasted_iota(I32, (128, 128), 0) < lax.broadcasted_iota(I32, (128, 128), 1),
                         1.0, 0.0).astype(BF16)
        ones = jnp.ones((128, 128), BF16)
        carry = jnp.zeros((QR, 128), F32)
        for s in range(nslab):
            x = key_scr[:, s * 128:(s + 1) * 128]
            eq = x == thr
            eqb = jnp.where(eq, 1.0, 0.0).astype(BF16)
            rank = jnp.dot(eqb, triu, preferred_element_type=F32) + carry
            mb = jnp.where(x > thr, 0.0, jnp.where(eq, jnp.where(rank < need, 0.0, NEG), NEG))
            if s == nslab - 1:
                mb = jnp.where(new_ok, mb, NEG)
            mb_ref[:, s * 128:(s + 1) * 128] = mb
            carry = carry + jnp.dot(eqb, ones, preferred_element_type=F32)


def _dsa_s_select(page_table, qi32, w32, ki_new_t, kidx_t, ksel, t_new):
    nb, n_pages = page_table.shape
    nj = n_pages // PG
    Lp = (n_pages + 1) * PAGE_SIZE

    def page_spec(p):
        return pl.BlockSpec((None, IDX_DIM, PAGE_SIZE), lambda b, j, pt: (pt[b, j * PG + p], 0, 0))

    grid_spec = pltpu.PrefetchScalarGridSpec(
        num_scalar_prefetch=1, grid=(nb, nj),
        in_specs=[pl.BlockSpec((None, IDX_HEADS * QR, IDX_DIM), lambda b, j, pt: (b, 0, 0)),
                  pl.BlockSpec((None, IDX_HEADS * QR, 128), lambda b, j, pt: (b, 0, 0)),
                  pl.BlockSpec((None, IDX_DIM, PAGE_SIZE), lambda b, j, pt: (b, 0, 0))]
                 + [page_spec(p) for p in range(PG)],
        out_specs=pl.BlockSpec((None, QR, Lp), lambda b, j, pt: (b, 0, 0)),
        scratch_shapes=[pltpu.VMEM((QR, Lp), I32)])
    return pl.pallas_call(
        functools.partial(_dsa_s_select_body, n_pages=n_pages, ksel=ksel, t_new=t_new),
        grid_spec=grid_spec,
        out_shape=jax.ShapeDtypeStruct((nb, QR, Lp), F32),
        compiler_params=_cparams(("parallel", "arbitrary")), name="dsa_sample_select",
    )(page_table, qi32, w32, ki_new_t, *([kidx_t] * PG))


def _dsa_s_attn_body(pt_ref, q_ref, mb_ref, mbn_ref, kvn_ref, bias_ref, *rest):
    kpages = rest[:PG]
    vpages = rest[PG:2 * PG]
    o_ref = rest[2 * PG]
    m_scr, l_scr, acc_scr = rest[2 * PG + 1:]
    j = pl.program_id(1)
    last = j == pl.num_programs(1) - 1
    scale = (B_HEAD_DIM ** -0.5) * LOG2E

    @pl.when(j == 0)
    def _init():
        m_scr[...] = jnp.full(m_scr.shape, NEG, F32)
        l_scr[...] = jnp.zeros(l_scr.shape, F32)
        acc_scr[...] = jnp.zeros(acc_scr.shape, F32)

    def near_bias(g, kind):
        return jnp.concatenate([bias_ref[g * B_GROUP + r, kind] for r in range(B_GROUP)], axis=0)

    def update(g, s, vs):
        m_old = m_scr[g]
        m_new = m_old
        for t in s:
            m_new = jnp.maximum(m_new, jnp.max(t, axis=1, keepdims=True))
        alpha = jnp.exp2(m_old - m_new)
        l = alpha * l_scr[g]
        acc = alpha * acc_scr[g]
        for t, vt in zip(s, vs):
            p = jnp.exp2(t - m_new)
            l = l + jnp.sum(p, axis=1, keepdims=True)
            acc = acc + jnp.dot(p.astype(BF16), vt, preferred_element_type=F32)
        m_scr[g] = m_new
        l_scr[g] = l
        acc_scr[g] = acc

    for g in range(B_KV_HEADS):
        qg = (q_ref[g] * scale).astype(BF16)
        tiles, vals = [], []
        for p in range(PG):
            kp = kpages[p][pl.ds(g, PAGE_SIZE, stride=B_KV_HEADS), :].astype(BF16)
            s = lax.dot_general(qg, kp, NT_DIMS, preferred_element_type=F32)
            mb = mb_ref[:, p * PAGE_SIZE:(p + 1) * PAGE_SIZE]
            s = s + jnp.concatenate([mb] * B_GROUP, axis=0)
            if p == PG - 1:
                s = s + jnp.where(last, near_bias(g, 0), 0.0)
            tiles.append(s)
            vals.append(vpages[p][pl.ds(g, PAGE_SIZE, stride=B_KV_HEADS), :].astype(BF16))
        update(g, tiles, vals)

    @pl.when(last)
    def _fin():
        for g in range(B_KV_HEADS):
            qg = (q_ref[g] * scale).astype(BF16)
            lo, hi = g * B_HEAD_DIM, (g + 1) * B_HEAD_DIM
            kn = kvn_ref[:, lo:hi].astype(BF16)
            vn = kvn_ref[:, B_KV_HEADS * B_HEAD_DIM + lo:B_KV_HEADS * B_HEAD_DIM + hi].astype(BF16)
            s = lax.dot_general(qg, kn, NT_DIMS, preferred_element_type=F32)
            s = s + jnp.concatenate([mbn_ref[...]] * B_GROUP, axis=0) + near_bias(g, 1)
            update(g, [s], [vn])
            o_ref[g] = acc_scr[g] * (1.0 / l_scr[g])


def _dsa_s_attn(page_table, q32, mb, kv_new, bias_s, ck_rows, cv_rows):
    nb, n_pages = page_table.shape
    nj = n_pages // PG
    R = B_GROUP * QR
    PR = PAGE_SIZE * B_KV_HEADS

    def page_spec(p):
        return pl.BlockSpec((PR, B_HEAD_DIM), lambda b, j, pt: (pt[b, j * PG + p], 0))

    grid_spec = pltpu.PrefetchScalarGridSpec(
        num_scalar_prefetch=1, grid=(nb, nj),
        in_specs=[pl.BlockSpec((None, B_KV_HEADS, R, B_HEAD_DIM), lambda b, j, pt: (b, 0, 0, 0)),
                  pl.BlockSpec((None, QR, PG * PAGE_SIZE), lambda b, j, pt: (b, 0, j)),
                  pl.BlockSpec((None, QR, PAGE_SIZE), lambda b, j, pt: (b, 0, n_pages)),
                  pl.BlockSpec((None, PAGE_SIZE, 2 * B_KV_HEADS * B_HEAD_DIM), lambda b, j, pt: (b, 0, 0)),
                  pl.BlockSpec(bias_s.shape, lambda b, j, pt: (0, 0, 0, 0))]
                 + [page_spec(p) for p in range(PG)] + [page_spec(p) for p in range(PG)],
        out_specs=pl.BlockSpec((None, B_KV_HEADS, R, B_HEAD_DIM), lambda b, j, pt: (b, 0, 0, 0)),
        scratch_shapes=[pltpu.VMEM((B_KV_HEADS, R, 1), F32), pltpu.VMEM((B_KV_HEADS, R, 1), F32),
                        pltpu.VMEM((B_KV_HEADS, R, B_HEAD_DIM), F32)])
    return pl.pallas_call(
        _dsa_s_attn_body, grid_spec=grid_spec,
        out_shape=jax.ShapeDtypeStruct((nb, B_KV_HEADS, R, B_HEAD_DIM), F32),
        compiler_params=_cparams(("parallel", "arbitrary")), name="dsa_sample_attn",
    )(page_table, q32, mb, mb, kv_new, bias_s, *([ck_rows] * PG), *([cv_rows] * PG))


def _layer_norm(r, g, b):
    mu = jnp.mean(r, axis=-1, keepdims=True)
    d = r - mu
    var = jnp.mean(d * d, axis=-1, keepdims=True)
    return d * lax.rsqrt(var + LN_EPS) * g + b


def _merge_body(oa_ref, ob_ref, gates_ref, x_ref, g1_ref, sh2_ref, sc2_ref, w_ref, lng_ref, lnb_ref,
                x1_ref, h2_ref, *, alpha):
    D = oa_ref.shape[1]
    merged = _sigmoid(gates_ref[:, 0:D]) * oa_ref[...] + _sigmoid(gates_ref[:, D:2 * D]) * ob_ref[...]
    y = jnp.dot(merged.astype(BF16), w_ref[...], preferred_element_type=F32)
    x1 = _layer_norm(alpha * x_ref[...] + g1_ref[...] * y, lng_ref[...], lnb_ref[...])
    x1_ref[...] = x1
    h2_ref[...] = (x1 * (1.0 + sc2_ref[...]) + sh2_ref[...]).astype(BF16)


def _merge(oa, ob, z, x2d, mod, rows_per_mod, w_out_bf, ln_g, ln_b, alpha, tm):
    N, D = x2d.shape
    row = lambda i: (i, 0)
    const = lambda i: (0, 0)
    return pl.pallas_call(
        functools.partial(_merge_body, alpha=alpha), grid=(N // tm,),
        in_specs=[pl.BlockSpec((tm, D), row), pl.BlockSpec((tm, D), row),
                  pl.BlockSpec((tm, 2 * D), lambda i: (i, Z_GATES // (2 * D))),
                  pl.BlockSpec((tm, D), row),
                  _mod_spec(mod, 2, tm, rows_per_mod, D, 1),
                  _mod_spec(mod, 3, tm, rows_per_mod, D, 1),
                  _mod_spec(mod, 4, tm, rows_per_mod, D, 1),
                  pl.BlockSpec((D, D), const), pl.BlockSpec((1, D), const), pl.BlockSpec((1, D), const)],
        out_specs=[pl.BlockSpec((tm, D), row), pl.BlockSpec((tm, D), row)],
        out_shape=[jax.ShapeDtypeStruct((N, D), F32), jax.ShapeDtypeStruct((N, D), BF16)],
        compiler_params=_cparams(("parallel",)), name="merge_out",
    )(oa, ob, z, x2d, mod, mod, mod, w_out_bf, ln_g, ln_b)


def _top_values(x, n):
    out = []
    for r in range(n):
        m = jnp.max(x, axis=0, keepdims=True)
        out.append(m)
        if r < n - 1:
            x = jnp.where(x == m, -jnp.inf, x)
    return out


_PEER_PAIRS = [(a, b) for a in range(PEER_TOPK) for b in range(PEER_TOPK) if (a + 1) * (b + 1) <= PEER_TOPK]


def _peer_body(h2_ref, x1_ref, g2_ref, wq_ref, sk_ref, u0_ref, un_ref, vt_ref, lng_ref, lnb_ref, o_ref,
               cut_scr, s2_scr, e1_scr, e2_scr, acc_scr, act_ab, w_ab, *, tn, ec, nchunks, alpha):
    e = pl.program_id(1)
    ne = pl.num_programs(1) - 1
    n_i1 = ec // N_KEYS

    @pl.when(e == 0)
    def _select():
        q = jnp.dot(h2_ref[...], wq_ref[...], preferred_element_type=F32)
        for h in range(PEER_HEADS):
            sc, tops = [], []
            for c in range(2):
                qhc = q[:, (2 * h + c) * N_KEYS:(2 * h + c + 1) * N_KEYS].astype(BF16)
                sT = lax.dot_general(sk_ref[c], qhc, NT_DIMS, preferred_element_type=F32)
                sc.append(sT)
                tops.append(_top_values(sT, PEER_TOPK))
            (s1, s2), (t1, t2) = sc, tops
            cand = jnp.concatenate([t1[a] + t2[b] for a, b in _PEER_PAIRS], axis=0)
            thr = _top_values(cand, PEER_TOPK)[-1]
            mx = t1[0] + t2[0]
            zsum = jnp.sum(jnp.where(cand >= thr, jnp.exp(cand - mx), 0.0), axis=0, keepdims=True)
            cut = jnp.full(s1.shape, jnp.inf, F32)
            for r in range(PEER_TOPK):
                cut = jnp.where(s1 + t2[r] >= thr, t2[r], cut)
            cut_scr[h] = cut
            s2_scr[h] = s2
            e1_scr[h] = jnp.exp(s1 - t1[0]) * (1.0 / zsum)
            e2_scr[h] = jnp.exp(s2 - t2[0])
        act_ab[0] = lax.dot_general(u0_ref[...], h2_ref[...], NT_DIMS, preferred_element_type=F32)
        w_ab[1] = jnp.zeros(w_ab.shape[1:], BF16)
        acc_scr[...] = jnp.zeros(acc_scr.shape, F32)

    def chunk(act_cur, act_nxt, w_cur, w_prev):
        acc_scr[...] += jnp.dot(vt_ref[...], w_prev[...], preferred_element_type=F32)
        act_nxt[...] = lax.dot_general(un_ref[...], h2_ref[...], NT_DIMS, preferred_element_type=F32)
        for jj in range(n_i1):
            i1 = e * n_i1 + jj
            a = act_cur[jj * N_KEYS:(jj + 1) * N_KEYS, :]
            gsum = jnp.zeros((N_KEYS, tn), F32)
            for h in range(PEER_HEADS):
                sel = s2_scr[h] >= cut_scr[h, pl.ds(i1, 1), :]
                gsum = gsum + jnp.where(sel, e2_scr[h], 0.0) * e1_scr[h, pl.ds(i1, 1), :]
            gelu = 0.5 * a * (1.0 + lax.erf(a * (2.0 ** -0.5)))
            w_cur[jj * N_KEYS:(jj + 1) * N_KEYS, :] = (gsum * gelu).astype(BF16)

    @pl.when(e < ne)
    def _chunk():
        slot = e % 2
        chunk(act_ab.at[slot], act_ab.at[1 - slot], w_ab.at[slot], w_ab.at[1 - slot])

    @pl.when(e == ne)
    def _fin():
        acc = acc_scr[...] + jnp.dot(vt_ref[...], w_ab[(nchunks - 1) % 2], preferred_element_type=F32)
        y = acc.T
        o_ref[...] = _layer_norm(alpha * x1_ref[...] + g2_ref[...] * y, lng_ref[...], lnb_ref[...])


def _peer(h2, x1, mod, rows_per_mod, wq_bf, sk_bf, u_bf, vt_bf, ln_g, ln_b, alpha, tn, ec):
    N, D = x1.shape
    E = u_bf.shape[0]
    ne = E // ec
    row = lambda i, e: (i, 0)
    const2 = lambda i, e: (0, 0)
    if mod.ndim == 3:
        g2_spec = pl.BlockSpec((None, 1, D), lambda i, e: ((i * tn) // rows_per_mod, 0, 5))
    else:
        g2_spec = pl.BlockSpec((tn, D), lambda i, e: (i, 5))
    return pl.pallas_call(
        functools.partial(_peer_body, tn=tn, ec=ec, nchunks=ne, alpha=alpha), grid=(N // tn, ne + 1),
        in_specs=[pl.BlockSpec((tn, D), row), pl.BlockSpec((tn, D), row), g2_spec,
                  _resident(wq_bf.shape, const2),
                  _resident(sk_bf.shape, lambda i, e: (0, 0, 0)),
                  _resident((ec, D), const2),
                  pl.BlockSpec((ec, D), lambda i, e: (jnp.minimum(e + 1, ne - 1), 0)),
                  pl.BlockSpec((D, ec), lambda i, e: (0, jnp.maximum(e - 1, 0))),
                  pl.BlockSpec((1, D), const2), pl.BlockSpec((1, D), const2)],
        out_specs=pl.BlockSpec((tn, D), row),
        out_shape=jax.ShapeDtypeStruct((N, D), F32),
        scratch_shapes=[pltpu.VMEM((PEER_HEADS, N_KEYS, tn), F32), pltpu.VMEM((PEER_HEADS, N_KEYS, tn), F32),
                        pltpu.VMEM((PEER_HEADS, N_KEYS, tn), F32), pltpu.VMEM((PEER_HEADS, N_KEYS, tn), F32),
                        pltpu.VMEM((D, tn), F32),
                        pltpu.VMEM((2, ec, tn), F32), pltpu.VMEM((2, ec, tn), BF16)],
        compiler_params=_cparams(("parallel", "arbitrary")), name="peer",
    )(h2, x1, mod, wq_bf, sk_bf, u_bf, u_bf, vt_bf, ln_g, ln_b)


def _pack_w_in(w_in):
    D = w_in.shape[0]
    idx_w = w_in[:, 5632:5956]
    pad = jnp.zeros((D, Z_GATES - Z_QI - idx_w.shape[1]), w_in.dtype)
    return jnp.concatenate([w_in[:, :5632], idx_w, pad, w_in[:, 5956:]], axis=1).astype(BF16)


def _prompt_buckets(tq):
    i = np.arange(tq)[:, None]
    j = np.arange(tq)[None, :]
    return np.stack([_t5_bucket_np(np.maximum(i - j, 0)), _t5_bucket_np(tq + i - j)])


def _sample_buckets():
    t = np.arange(QR)[:, None]
    o = np.arange(PAGE_SIZE)[None, :]
    return np.stack([_t5_bucket_np(PAGE_SIZE + t - o), _t5_bucket_np(np.maximum(t - o, 0))])


def _layer_tail(oa, ob, z, x2d, mod, rows_per_mod, wts, alpha, tm, tn_peer):
    x1, h2 = _merge(oa, ob, z, x2d, mod, rows_per_mod, wts["w_out"], wts["ln1_g"], wts["ln1_b"], alpha, tm)
    return _peer(h2, x1, mod, rows_per_mod, wts["w_peer_q"], wts["sub_keys"], wts["peer_u"], wts["peer_vt"],
                 wts["ln2_g"], wts["ln2_b"], alpha, tn_peer, 1024)


def kernel(x_prompt, x_sample, cache_k, cache_v, cache_kidx, state_hgrn, page_table, c_prompt, c_sample,
           w_ada, b_ada, w_in, lb_param, a_gnorm, t5_table, w_out, ln1_g, ln1_b, w_peer_q, peer_sub_keys,
           peer_u, peer_v, ln2_g, ln2_b):
    depth = w_ada.shape[0]
    assert depth == 1 and lb_param.shape[0] == 2
    alpha = (2.0 * depth) ** 0.25
    B, T, D = x_prompt.shape
    NB, TS, _ = x_sample.shape
    n_pages = page_table.shape[1]
    past_len = n_pages * PAGE_SIZE

    wts = dict(
        w_out=w_out[0].astype(BF16), ln1_g=ln1_g, ln1_b=ln1_b, ln2_g=ln2_g, ln2_b=ln2_b,
        w_peer_q=w_peer_q[0].astype(BF16), sub_keys=peer_sub_keys[0].astype(BF16),
        peer_u=peer_u[0].astype(BF16), peer_vt=peer_v[0].T.astype(BF16))
    w_in_bf = _pack_w_in(w_in[0])
    gnorm = a_gnorm.reshape(1, A_DK)

    mod = _ada(jnp.concatenate([c_prompt, c_sample], axis=0), w_ada[0].astype(BF16), b_ada)
    mod_p = mod[:B].reshape(B, 1, 6 * D)
    mod_s = jnp.repeat(mod[B:], TS, axis=0)

    xp = x_prompt.reshape(B * T, D)
    zp = _inproj(xp, mod_p, T, w_in_bf, 512, 1024)
    oa_p, s_p = _hgrn(zp, jnp.zeros((B, A_HEADS, A_DK, A_DK), F32), lb_param, gnorm, B, T, 256, A_CHUNK, 256, 4)
    tq = 256
    bias_p = _bias_tiles(t5_table, _prompt_buckets(tq))
    bias_g = bias_p.reshape(B_KV_HEADS, B_GROUP, 2, tq, tq).transpose(0, 2, 1, 3, 4)
    bias_g = bias_g.reshape(B_KV_HEADS, 2, B_GROUP * tq, tq)
    bias_g = jnp.concatenate([bias_g, jnp.zeros((B_KV_HEADS, 1, B_GROUP * tq, tq), F32)], axis=1)
    ob_p = _dsa_prompt(zp, bias_g, B, T, tq)
    y_p = _layer_tail(oa_p, ob_p, zp, xp, mod_p, T, wts, alpha, 256, 512)

    xs = x_sample.reshape(NB * TS, D)
    zs = _inproj(xs, mod_s, 1, w_in_bf, NB * TS, 1024)
    zs3 = zs.reshape(NB, TS, Z_WIDTH)
    zs_pad = jnp.pad(zs3, ((0, 0), (0, QR - TS), (0, 0))).reshape(NB * QR, Z_WIDTH)
    assert TS <= QR
    oa_s8, s_s = _hgrn(zs_pad, state_hgrn[0], lb_param, gnorm, NB, QR, QR, QR, TS, A_HEADS)
    oa_s = oa_s8.reshape(NB, QR, D)[:, :TS].reshape(NB * TS, D)

    qi_s = zs3[:, :, Z_QI:Z_QI + IDX_HEADS * IDX_DIM].reshape(NB, TS, IDX_HEADS, IDX_DIM)
    qi32 = jnp.pad(qi_s.transpose(0, 2, 1, 3), ((0, 0), (0, 0), (0, QR - TS), (0, 0)))
    qi32 = qi32.reshape(NB, IDX_HEADS * QR, IDX_DIM).astype(BF16)
    wi_s = zs3[:, :, Z_WI:Z_WI + IDX_HEADS].transpose(0, 2, 1)
    w32 = jnp.pad(wi_s, ((0, 0), (0, 0), (0, QR - TS))).reshape(NB, IDX_HEADS * QR, 1)
    w32 = jnp.broadcast_to(w32, (NB, IDX_HEADS * QR, 128))
    ki_new = jnp.pad(zs3[:, :, Z_KI:Z_KI + IDX_DIM], ((0, 0), (0, PAGE_SIZE - TS), (0, 0))).astype(BF16)
    ki_new_t = jnp.swapaxes(ki_new, 1, 2)
    kv_new = jnp.pad(zs3[:, :, Z_KB:Z_KB + 512], ((0, 0), (0, PAGE_SIZE - TS), (0, 0)))
    q_s = zs3[:, :, Z_QB:Z_QB + 1024].reshape(NB, TS, B_KV_HEADS, B_GROUP, B_HEAD_DIM)
    q32 = jnp.pad(q_s.transpose(0, 2, 3, 1, 4), ((0, 0), (0, 0), (0, 0), (0, QR - TS), (0, 0)))
    q32 = q32.reshape(NB, B_KV_HEADS, B_GROUP * QR, B_HEAD_DIM)

    ksel = min(TOPK_MAX, (past_len + TS) // 4)
    kidx_t = jnp.swapaxes(cache_kidx[0], 1, 2)
    mb_s = _dsa_s_select(page_table, qi32, w32, ki_new_t, kidx_t, ksel, TS)
    bias_s = _bias_tiles(t5_table, _sample_buckets())
    n_pool = cache_k.shape[1]
    ck = cache_k.reshape(n_pool * PAGE_SIZE * B_KV_HEADS, B_HEAD_DIM)
    cv = cache_v.reshape(n_pool * PAGE_SIZE * B_KV_HEADS, B_HEAD_DIM)
    o32 = _dsa_s_attn(page_table, q32, mb_s, kv_new, bias_s, ck, cv)
    ob_s = o32.reshape(NB, B_KV_HEADS, B_GROUP, QR, B_HEAD_DIM)[:, :, :, :TS]
    ob_s = ob_s.transpose(0, 3, 1, 2, 4).reshape(NB * TS, D)
    y_s = _layer_tail(oa_s, ob_s, zs, xs, mod_s, 1, wts, alpha, NB * TS, NB * TS)

    def kv_out(z, nb, t):
        k = z[:, Z_KB:Z_KB + 256].reshape(1, nb, t, B_KV_HEADS, B_HEAD_DIM)
        v = z[:, Z_VB:Z_VB + 256].reshape(1, nb, t, B_KV_HEADS, B_HEAD_DIM)
        ki = z[:, Z_KI:Z_KI + IDX_DIM].reshape(1, nb, t, IDX_DIM)
        return k, v, ki

    kp, vp, ip = kv_out(zp, B, T)
    ks, vs, is_ = kv_out(zs, NB, TS)
    return (y_p.reshape(B, T, D), y_s.reshape(NB, TS, D), kp, vp, ip, s_p[None],
            ks, vs, is_, s_s[None])
```

```python
import functools
import math

import numpy as np
import jax
import jax.numpy as jnp
from jax import lax
from jax.experimental import pallas as pl
from jax.experimental.pallas import tpu as pltpu

F32 = jnp.float32
BF16 = jnp.bfloat16
I32 = jnp.int32

A_HEADS = 8
A_DK = 128
A_CHUNK = 32
B_HEADS = 8
B_KV_HEADS = 2
B_GROUP = 4
B_HEAD_DIM = 128
IDX_HEADS = 4
IDX_DIM = 64
TOPK_MAX = 256
N_BUCKETS = 32
MAX_DISTANCE = 128
PEER_HEADS = 8
N_KEYS = 128
PEER_TOPK = 16
PAGE_SIZE = 128
LN_EPS = 1e-5
RMS_EPS = 1e-6

Z_A = 0
Z_QB = 4096
Z_KB = 5120
Z_VB = 5376
Z_QI = 5632
Z_KI = 5888
Z_WI = 5952
Z_GATES = 6144
Z_WIDTH = 8192

NEG = -1e30
INT_MIN = -2 ** 31
LOG2E = 1.4426950408889634
NT_DIMS = (((1,), (1,)), ((), ()))
VMEM_LIMIT = 56 * 1024 * 1024


def _cparams(sem):
    return pltpu.CompilerParams(dimension_semantics=sem, vmem_limit_bytes=VMEM_LIMIT)


def _resident(shape, index_map):
    return pl.BlockSpec(shape, index_map, pipeline_mode=pl.Buffered(1))


def _sigmoid(x):
    return jax.nn.sigmoid(x)


def _ada_body(c_ref, w_ref, b_ref, o_ref):
    c = c_ref[...]
    s = c * _sigmoid(c)
    o_ref[...] = jnp.dot(s.astype(BF16), w_ref[...], preferred_element_type=F32) + b_ref[...]


def _ada(c, w_bf, b):
    R, D = c.shape
    N = w_bf.shape[1]
    tn = N // 4
    return pl.pallas_call(
        _ada_body, grid=(N // tn,),
        in_specs=[pl.BlockSpec((R, D), lambda j: (0, 0)),
                  pl.BlockSpec((D, tn), lambda j: (0, j)),
                  pl.BlockSpec((1, tn), lambda j: (0, j))],
        out_specs=pl.BlockSpec((R, tn), lambda j: (0, j)),
        out_shape=jax.ShapeDtypeStruct((R, N), F32),
        compiler_params=_cparams(("parallel",)), name="ada")(c, w_bf, b)


def _mod_spec(mod, blk, tm, rows_per_mod, D, nidx):
    if mod.ndim == 3:
        if nidx == 1:
            return pl.BlockSpec((None, 1, D), lambda i: ((i * tm) // rows_per_mod, 0, blk))
        return pl.BlockSpec((None, 1, D), lambda i, j: ((i * tm) // rows_per_mod, 0, blk))
    if nidx == 1:
        return pl.BlockSpec((tm, D), lambda i: (i, blk))
    return pl.BlockSpec((tm, D), lambda i, j: (i, blk))


def _inproj_body(x_ref, sh_ref, sc_ref, w_ref, o_ref, *, tn):
    h = (x_ref[...] * (1.0 + sc_ref[...]) + sh_ref[...]).astype(BF16)
    for j in range(o_ref.shape[1] // tn):
        o_ref[:, j * tn:(j + 1) * tn] = jnp.dot(h, w_ref[:, j * tn:(j + 1) * tn], preferred_element_type=F32)


def _inproj(x2d, mod, rows_per_mod, w_bf, tm, tn):
    N, D = x2d.shape
    W = w_bf.shape[1]
    return pl.pallas_call(
        functools.partial(_inproj_body, tn=tn), grid=(N // tm,),
        in_specs=[pl.BlockSpec((tm, D), lambda i: (i, 0)),
                  _mod_spec(mod, 0, tm, rows_per_mod, D, 1),
                  _mod_spec(mod, 1, tm, rows_per_mod, D, 1),
                  _resident((D, W), lambda i: (0, 0))],
        out_specs=pl.BlockSpec((tm, W), lambda i: (i, 0)),
        out_shape=jax.ShapeDtypeStruct((N, W), F32),
        compiler_params=_cparams(("parallel",)), name="inproj")(x2d, mod, mod, w_bf)


def _hgrn_body(q_ref, f_ref, i_ref, g_ref, lbp_ref, gn_ref, s0_ref, o_ref, sout_ref, st_scr,
               *, C, Tb, t_valid, HB):
    tb = pl.program_id(2)
    shift = int(math.log2(C))
    nchunk = Tb // C
    r = lax.broadcasted_iota(I32, (Tb, Tb), 0)
    c = lax.broadcasted_iota(I32, (Tb, Tb), 1)
    tril = (jnp.right_shift(r, shift) == jnp.right_shift(c, shift)) & (c <= r)
    trilf = jnp.where(tril, 1.0, 0.0)
    Tp = max(Tb, 128)
    colchunk = jnp.right_shift(lax.broadcasted_iota(I32, (A_DK, Tp), 1), shift)

    lbp = lbp_ref[...]
    mx = jnp.max(lbp, axis=0, keepdims=True)
    e = jnp.exp(lbp - mx)
    lb_all = e[0:1, :] / jnp.sum(e, axis=0, keepdims=True)

    @pl.when(tb == 0)
    def _init():
        for hh in range(HB):
            st_scr[hh] = s0_ref[hh].T

    W = HB * A_DK
    f = lb_all + (1.0 - lb_all) * _sigmoid(f_ref[...])
    lf = jnp.log(f)
    k = 1.0 - f
    if t_valid < Tb:
        ok = lax.broadcasted_iota(I32, (Tb, W), 0) < t_valid
        lf = jnp.where(ok, lf, 0.0)
        k = jnp.where(ok, k, 0.0)
    bcum = jnp.dot(trilf, lf, precision=lax.Precision.HIGHEST,
                   preferred_element_type=F32)
    tot = jnp.concatenate(
        [jnp.broadcast_to(bcum[(j + 1) * C - 1:(j + 1) * C, :], (C, W)) for j in range(nchunk)], axis=0)
    qa = q_ref[...]
    qe_b = ((qa * _sigmoid(qa)) * jnp.exp(bcum)).astype(BF16)
    ke_b = (k * jnp.exp(-bcum)).astype(BF16)
    kd = k * jnp.exp(tot - bcum)
    v = i_ref[...]
    v_b = v.astype(BF16)
    if Tb < 128:
        zpad = jnp.zeros((128 - Tb, W), F32)
        v_p = jnp.concatenate([v, zpad], axis=0)
        kd_p = jnp.concatenate([kd, zpad], axis=0)
    else:
        v_p, kd_p = v, kd
    kd_b = kd_p.astype(BF16)
    heads = [(hh * A_DK, (hh + 1) * A_DK) for hh in range(HB)]

    scores = [jnp.where(tril, lax.dot_general(qe_b[:, lo:hi], ke_b[:, lo:hi], NT_DIMS,
                                              preferred_element_type=F32), 0.0).astype(BF16) for lo, hi in heads]
    o = [jnp.dot(scores[hh], v_b[:, lo:hi], preferred_element_type=F32) for hh, (lo, hi) in enumerate(heads)]
    vT = [v_p[:, lo:hi].T for lo, hi in heads]
    ST = [st_scr[hh] for hh in range(HB)]
    o_inter = [[] for _ in range(HB)]
    for j in range(nchunk):
        for hh, (lo, hi) in enumerate(heads):
            o_inter[hh].append(lax.dot_general(qe_b[j * C:(j + 1) * C, lo:hi], ST[hh].astype(BF16), NT_DIMS,
                                               preferred_element_type=F32))
            d_c = jnp.exp(tot[j * C:j * C + 1, lo:hi])
            vTm = jnp.where(colchunk == j, vT[hh], 0.0) if nchunk > 1 else vT[hh]
            ST[hh] = ST[hh] * d_c + jnp.dot(vTm.astype(BF16), kd_b[:, lo:hi], preferred_element_type=F32)
    gn = gn_ref[...]
    ga = g_ref[...]
    gate = ga * _sigmoid(ga)
    for hh, (lo, hi) in enumerate(heads):
        st_scr[hh] = ST[hh]
        oh = o[hh] + jnp.concatenate(o_inter[hh], axis=0)
        ms = jnp.mean(oh * oh, axis=-1, keepdims=True)
        o_ref[:, lo:hi] = oh * lax.rsqrt(ms + RMS_EPS) * gn * gate[:, lo:hi]

    @pl.when(tb == pl.num_programs(2) - 1)
    def _fin():
        for hh in range(HB):
            sout_ref[hh] = st_scr[hh].T


def _hgrn(z, s0, lb_param, gnorm, B, T, Tb, C, t_valid, HB):
    nt = T // Tb
    H = A_HEADS
    W = HB * A_DK
    ng = H // HB

    def col(group):
        return pl.BlockSpec((Tb, W), lambda b, h, t: (b * nt + t, group * ng + h))

    return pl.pallas_call(
        functools.partial(_hgrn_body, C=C, Tb=Tb, t_valid=t_valid, HB=HB),
        grid=(B, ng, nt),
        in_specs=[col(0), col(1), col(2), col(3),
                  pl.BlockSpec((2, W), lambda b, h, t: (0, h)),
                  pl.BlockSpec((1, A_DK), lambda b, h, t: (0, 0)),
                  pl.BlockSpec((None, HB, A_DK, A_DK), lambda b, h, t: (b, h, 0, 0))],
        out_specs=[pl.BlockSpec((Tb, W), lambda b, h, t: (b * nt + t, h)),
                   pl.BlockSpec((None, HB, A_DK, A_DK), lambda b, h, t: (b, h, 0, 0))],
        out_shape=[jax.ShapeDtypeStruct((B * T, H * A_DK), F32),
                   jax.ShapeDtypeStruct((B, H, A_DK, A_DK), F32)],
        scratch_shapes=[pltpu.VMEM((HB, A_DK, A_DK), F32)],
        compiler_params=_cparams(("parallel", "parallel", "arbitrary")), name="hgrn",
    )(z, z, z, z, lb_param, gnorm, s0)


def _t5_bucket_np(rel):
    rel = np.asarray(rel, np.int64)
    max_exact = N_BUCKETS // 2
    relf = np.maximum(rel, 1).astype(np.float64)
    large = max_exact + (np.log(relf / max_exact) / math.log(MAX_DISTANCE / max_exact)
                         * (N_BUCKETS - max_exact)).astype(np.int64)
    large = np.minimum(large, N_BUCKETS - 1)
    return np.where(rel < max_exact, rel, large).astype(np.int32)


def _bias_body(tab_ref, bkt_ref, o_ref):
    h = pl.program_id(0)
    b = bkt_ref[...]
    acc = jnp.zeros(b.shape, F32)
    for i in range(N_BUCKETS):
        acc = jnp.where(b == i, tab_ref[i, h], acc)
    o_ref[...] = (acc - tab_ref[N_BUCKETS - 1, h]) * LOG2E


def _bias_tiles(t5_table, buckets):
    K, R, Cc = buckets.shape
    return pl.pallas_call(
        _bias_body, grid=(B_HEADS, K),
        in_specs=[pl.BlockSpec(memory_space=pltpu.SMEM),
                  pl.BlockSpec((None, R, Cc), lambda h, k: (k, 0, 0))],
        out_specs=pl.BlockSpec((None, None, R, Cc), lambda h, k: (h, k, 0, 0)),
        out_shape=jax.ShapeDtypeStruct((B_HEADS, K, R, Cc), F32),
        compiler_params=_cparams(("parallel", "parallel")), name="t5_bias")(t5_table, jnp.asarray(buckets))


def _ordered_key(score):
    score = jnp.where(score == 0.0, 0.0, score)
    bits = pltpu.bitcast(score, I32)
    return jnp.where(bits < 0, jnp.bitwise_xor(bits, jnp.int32(0x7FFFFFFF)), bits)


def _bcast_lanes(col, n):
    return jnp.broadcast_to(col, (col.shape[0], n))


def _dsa_prompt_body(q_ref, kv_ref, idxq_ref, ki_ref, bias_ref, o_ref,
                     key_scr, mb_scr, m_scr, l_scr, acc_scr, s_scr, *, tq, ksel):
    qi = pl.program_id(1)
    nvalid = qi + 1
    npair = (nvalid + 1) // 2
    tw = 2 * tq

    idxq = idxq_ref[...]
    q_idx = [idxq[:, h * IDX_DIM:(h + 1) * IDX_DIM].astype(BF16) for h in range(IDX_HEADS)]
    w_t = idxq[:, Z_KI - Z_QI:Z_KI - Z_QI + 128].T
    w_rows = [w_t[Z_WI - Z_KI + h:Z_WI - Z_KI + h + 1, :] for h in range(IDX_HEADS)]
    key_i = lax.broadcasted_iota(I32, (tq, tq), 0)
    qry_t = qi * tq + lax.broadcasted_iota(I32, (tq, tq), 1)

    def idx_chunk(c, carry):
        off = pl.multiple_of(c * tq, tq)
        kic = ki_ref[pl.ds(off, tq), 0:IDX_DIM].astype(BF16)
        acc = jnp.zeros((tq, tq), F32)
        for h in range(IDX_HEADS):
            s = lax.dot_general(kic, q_idx[h], NT_DIMS, preferred_element_type=F32)
            acc = acc + jnp.maximum(s * (IDX_DIM ** -0.5), 0.0) * w_rows[h]
        score = acc * (IDX_HEADS ** -0.5)
        score = jnp.where(off + key_i <= qry_t, score, -jnp.inf)
        key_scr[pl.ds(off, tq), :] = _ordered_key(score)
        return carry

    lax.fori_loop(0, nvalid, idx_chunk, 0)

    @pl.when(nvalid % 2 == 1)
    def _pad():
        off = pl.multiple_of(nvalid * tq, tq)
        key_scr[pl.ds(off, tq), :] = jnp.full((tq, tq), INT_MIN, I32)
        mb_scr[:, pl.ds(off, tq)] = jnp.full((tq, tq), NEG, F32)

    NACC = 8

    def count(pred_fn):
        def body(c, cnt):
            off = pl.multiple_of(c * tw, tw)
            x = key_scr[pl.ds(off, tw), :].reshape(tw // (8 * NACC), NACC, 8, tq)
            return cnt + jnp.sum(jnp.where(pred_fn(x), 1.0, 0.0), axis=0)
        cnt = lax.fori_loop(0, npair, body, jnp.zeros((NACC, 8, tq), F32))
        return jnp.sum(jnp.sum(cnt, axis=0), axis=0, keepdims=True)

    def bit_step(it, thr):
        cand = thr + jnp.left_shift(jnp.int32(1), 31 - it)
        return jnp.where(count(lambda x: x >= cand) >= ksel, cand, thr)

    thr = lax.fori_loop(0, 32, bit_step, jnp.full((1, tq), INT_MIN, I32))
    need = ksel - count(lambda x: x > thr)
    tril = jnp.where(lax.broadcasted_iota(I32, (tq, tq), 1) < key_i, 1.0, 0.0).astype(BF16)

    def mask_chunk(c, carry):
        off = pl.multiple_of(c * tq, tq)
        x = key_scr[pl.ds(off, tq), :]
        eq = x == thr
        eqf = jnp.where(eq, 1.0, 0.0)
        rank = jnp.dot(tril, eqf.astype(BF16), preferred_element_type=F32) + carry
        mb = jnp.where(x > thr, 0.0, jnp.where(eq, jnp.where(rank < need, 0.0, NEG), NEG))
        mb = jnp.where(off + key_i <= qry_t, mb, NEG)
        mb_scr[:, pl.ds(off, tq)] = mb.T
        return carry + jnp.sum(eqf, axis=0, keepdims=True)

    lax.fori_loop(0, nvalid, mask_chunk, jnp.zeros((1, tq), F32))

    scale = (B_HEAD_DIM ** -0.5) * LOG2E
    nsl = tw // 128

    def bias_kind(c):
        d = qi - c
        return jnp.where(d == 0, 0, jnp.where(d == 1, 1, 2))

    for g in range(B_KV_HEADS):
        qg = jnp.concatenate(
            [q_ref[:, (g * B_GROUP + r) * B_HEAD_DIM:(g * B_GROUP + r + 1) * B_HEAD_DIM] for r in range(B_GROUP)],
            axis=0)
        qg = (qg * scale).astype(BF16)

        def logits(c2, g=g, qg=qg):
            off = pl.multiple_of(c2 * tw, tw)
            kc = kv_ref[pl.ds(off, tw), g * B_HEAD_DIM:(g + 1) * B_HEAD_DIM].astype(BF16)
            s = lax.dot_general(qg, kc, NT_DIMS, preferred_element_type=F32)
            mb = mb_scr[:, pl.ds(off, tw)]
            bias = jnp.concatenate([bias_ref[g, bias_kind(2 * c2)], bias_ref[g, bias_kind(2 * c2 + 1)]], axis=1)
            return jnp.concatenate([s[r * tq:(r + 1) * tq] + mb for r in range(B_GROUP)], axis=0) + bias

        def max_body(c2, m):
            s = logits(c2)
            s_scr[:, pl.ds(pl.multiple_of(c2 * tw, tw), tw)] = s
            for k in range(nsl):
                m = jnp.maximum(m, s[:, k * 128:(k + 1) * 128])
            return m

        m_scr[...] = lax.fori_loop(0, npair, max_body, jnp.full(m_scr.shape, NEG, F32))
        m_row = _bcast_lanes(jnp.max(m_scr[...], axis=1, keepdims=True), 128)
        m_scr[...] = m_row
        l_scr[...] = jnp.zeros(l_scr.shape, F32)
        acc_scr[...] = jnp.zeros(acc_scr.shape, F32)

        def sum_body(c2, carry, g=g):
            off = pl.multiple_of(c2 * tw, tw)
            vc = kv_ref[pl.ds(off, tw), (B_KV_HEADS + g) * B_HEAD_DIM:(B_KV_HEADS + g + 1) * B_HEAD_DIM].astype(BF16)
            m = m_scr[...]
            p = jnp.exp2(s_scr[:, pl.ds(off, tw)] - jnp.concatenate([m] * nsl, axis=1))
            l = l_scr[...]
            for k in range(nsl):
                l = l + p[:, k * 128:(k + 1) * 128]
            l_scr[...] = l
            acc_scr[...] += jnp.dot(p.astype(BF16), vc, preferred_element_type=F32)
            return carry

        lax.fori_loop(0, npair, sum_body, 0)
        out = acc_scr[...] * (1.0 / jnp.sum(l_scr[...], axis=1, keepdims=True))
        for r in range(B_GROUP):
            h = g * B_GROUP + r
            o_ref[:, h * B_HEAD_DIM:(h + 1) * B_HEAD_DIM] = out[r * tq:(r + 1) * tq]


def _dsa_prompt(z, bias_g, B, T, tq):
    nq = T // tq
    ksel = min(TOPK_MAX, T // 4)
    assert tq >= ksel and tq % 128 == 0 and nq % 2 == 0
    R = B_GROUP * tq
    return pl.pallas_call(
        functools.partial(_dsa_prompt_body, tq=tq, ksel=ksel),
        grid=(B, nq),
        in_specs=[pl.BlockSpec((tq, 1024), lambda b, i: (b * nq + i, Z_QB // 1024)),
                  _resident((T, 512), lambda b, i: (b, Z_KB // 512)),
                  pl.BlockSpec((tq, 512), lambda b, i: (b * nq + i, Z_QI // 512)),
                  _resident((T, 128), lambda b, i: (b, Z_KI // 128)),
                  _resident(bias_g.shape, lambda b, i: (0, 0, 0, 0))],
        out_specs=pl.BlockSpec((tq, 1024), lambda b, i: (b * nq + i, 0)),
        out_shape=jax.ShapeDtypeStruct((B * T, 1024), F32),
        scratch_shapes=[pltpu.VMEM((T, tq), I32), pltpu.VMEM((tq, T), F32),
                        pltpu.VMEM((R, 128), F32), pltpu.VMEM((R, 128), F32), pltpu.VMEM((R, B_HEAD_DIM), F32),
                        pltpu.VMEM((R, T), F32)],
        compiler_params=_cparams(("parallel", "arbitrary")), name="dsa_prompt",
    )(z, z, z, z, bias_g)


PG = 8
QR = 8


def _dsa_s_select_body(pt_ref, qi_ref, w_ref, kin_ref, *rest, n_pages, ksel, t_new):
    pages = rest[:PG]
    mb_ref = rest[PG]
    key_scr = rest[PG + 1]
    j = pl.program_id(1)
    L = n_pages * PAGE_SIZE
    nslab = n_pages + 1
    qi32 = qi_ref[...]
    w32 = w_ref[...]

    def chunk_score(keys_t_bf):
        s = jnp.dot(qi32, keys_t_bf, preferred_element_type=F32)
        s = jnp.maximum(s * (IDX_DIM ** -0.5), 0.0) * w32
        acc = s[0:QR]
        for h in range(1, IDX_HEADS):
            acc = acc + s[h * QR:(h + 1) * QR]
        return acc * (IDX_HEADS ** -0.5)

    for p in range(PG):
        off = pl.multiple_of((j * PG + p) * PAGE_SIZE, PAGE_SIZE)
        key_scr[:, pl.ds(off, PAGE_SIZE)] = _ordered_key(chunk_score(pages[p][...].astype(BF16)))

    @pl.when(j == pl.num_programs(1) - 1)
    def _select():
        row = lax.broadcasted_iota(I32, (QR, PAGE_SIZE), 0)
        col = lax.broadcasted_iota(I32, (QR, PAGE_SIZE), 1)
        new_ok = (col <= row) & (col < t_new)
        s_new = jnp.where(new_ok, chunk_score(kin_ref[...]), -jnp.inf)
        key_scr[:, L:L + PAGE_SIZE] = _ordered_key(s_new)

        def count(pred_fn):
            cnt = jnp.zeros((QR, 128), F32)
            for s in range(nslab):
                cnt = cnt + jnp.where(pred_fn(key_scr[:, s * 128:(s + 1) * 128]), 1.0, 0.0)
            return jnp.sum(cnt, axis=1, keepdims=True)

        def bit_step(it, thr):
            cand = thr + jnp.left_shift(jnp.int32(1), 31 - it)
            tot = count(lambda x: x >= cand)
            return jnp.where(_bcast_lanes(tot, 128) >= ksel, cand, thr)

        thr = lax.fori_loop(0, 32, bit_step, jnp.full((QR, 128), INT_MIN, I32))
        need = _bcast_lanes(ksel - count(lambda x: x > thr), 128)
        triu = jnp.where(lax.broadcasted_iota(I32, (128, 128), 0) < lax.broadcasted_iota(I32, (128, 128), 1),
                         1.0, 0.0).astype(BF16)
        ones = jnp.ones((128, 128), BF16)
        carry = jnp.zeros((QR, 128), F32)
        for s in range(nslab):
            x = key_scr[:, s * 128:(s + 1) * 128]
            eq = x == thr
            eqb = jnp.where(eq, 1.0, 0.0).astype(BF16)
            rank = jnp.dot(eqb, triu, preferred_element_type=F32) + carry
            mb = jnp.where(x > thr, 0.0, jnp.where(eq, jnp.where(rank < need, 0.0, NEG), NEG))
            if s == nslab - 1:
                mb = jnp.where(new_ok, mb, NEG)
            mb_ref[:, s * 128:(s + 1) * 128] = mb
            carry = carry + jnp.dot(eqb, ones, preferred_element_type=F32)


def _dsa_s_select(page_table, qi32, w32, ki_new_t, kidx_t, ksel, t_new):
    nb, n_pages = page_table.shape
    nj = n_pages // PG
    Lp = (n_pages + 1) * PAGE_SIZE

    def page_spec(p):
        return pl.BlockSpec((None, IDX_DIM, PAGE_SIZE), lambda b, j, pt: (pt[b, j * PG + p], 0, 0))

    grid_spec = pltpu.PrefetchScalarGridSpec(
        num_scalar_prefetch=1, grid=(nb, nj),
        in_specs=[pl.BlockSpec((None, IDX_HEADS * QR, IDX_DIM), lambda b, j, pt: (b, 0, 0)),
                  pl.BlockSpec((None, IDX_HEADS * QR, 128), lambda b, j, pt: (b, 0, 0)),
                  pl.BlockSpec((None, IDX_DIM, PAGE_SIZE), lambda b, j, pt: (b, 0, 0))]
                 + [page_spec(p) for p in range(PG)],
        out_specs=pl.BlockSpec((None, QR, Lp), lambda b, j, pt: (b, 0, 0)),
        scratch_shapes=[pltpu.VMEM((QR, Lp), I32)])
    return pl.pallas_call(
        functools.partial(_dsa_s_select_body, n_pages=n_pages, ksel=ksel, t_new=t_new),
        grid_spec=grid_spec,
        out_shape=jax.ShapeDtypeStruct((nb, QR, Lp), F32),
        compiler_params=_cparams(("parallel", "arbitrary")), name="dsa_sample_select",
    )(page_table, qi32, w32, ki_new_t, *([kidx_t] * PG))


def _dsa_s_attn_body(pt_ref, q_ref, mb_ref, mbn_ref, kvn_ref, bias_ref, *rest):
    kpages = rest[:PG]
    vpages = rest[PG:2 * PG]
    o_ref = rest[2 * PG]
    m_scr, l_scr, acc_scr = rest[2 * PG + 1:]
    j = pl.program_id(1)
    last = j == pl.num_programs(1) - 1
    scale = (B_HEAD_DIM ** -0.5) * LOG2E

    @pl.when(j == 0)
    def _init():
        m_scr[...] = jnp.full(m_scr.shape, NEG, F32)
        l_scr[...] = jnp.zeros(l_scr.shape, F32)
        acc_scr[...] = jnp.zeros(acc_scr.shape, F32)

    R = B_GROUP * QR
    nrep = B_KV_HEADS * B_GROUP
    qs = [(q_ref[g] * scale).astype(BF16) for g in range(B_KV_HEADS)]

    def near_bias(kind):
        return jnp.concatenate([bias_ref[h, kind] for h in range(B_HEADS)], axis=0)

    def update(s, vs):
        m_old = m_scr[...]
        m_new = jnp.maximum(m_old, jnp.max(s, axis=1, keepdims=True))
        alpha = jnp.exp2(m_old - m_new)
        p = jnp.exp2(s - m_new)
        l_scr[...] = alpha * l_scr[...] + jnp.sum(p, axis=1, keepdims=True)
        pb = p.astype(BF16)
        pv = jnp.concatenate([jnp.dot(pb[g * R:(g + 1) * R], vs[g], preferred_element_type=F32)
                              for g in range(B_KV_HEADS)], axis=0)
        acc_scr[...] = alpha * acc_scr[...] + pv
        m_scr[...] = m_new

    ks = [jnp.concatenate([kpages[p][pl.ds(g, PAGE_SIZE, stride=B_KV_HEADS), :].astype(BF16)
                           for p in range(PG)], axis=0) for g in range(B_KV_HEADS)]
    vs = [jnp.concatenate([vpages[p][pl.ds(g, PAGE_SIZE, stride=B_KV_HEADS), :].astype(BF16)
                           for p in range(PG)], axis=0) for g in range(B_KV_HEADS)]
    s = jnp.concatenate([lax.dot_general(qs[g], ks[g], NT_DIMS, preferred_element_type=F32)
                         for g in range(B_KV_HEADS)], axis=0)
    s = s + jnp.concatenate([mb_ref[...]] * nrep, axis=0)
    nb0 = jnp.where(last, near_bias(0), 0.0)
    s = jnp.concatenate([s[:, :(PG - 1) * PAGE_SIZE], s[:, (PG - 1) * PAGE_SIZE:] + nb0], axis=1)
    update(s, vs)

    @pl.when(last)
    def _fin():
        CW = B_KV_HEADS * B_HEAD_DIM
        kn = [kvn_ref[:, g * B_HEAD_DIM:(g + 1) * B_HEAD_DIM].astype(BF16) for g in range(B_KV_HEADS)]
        vn = [kvn_ref[:, CW + g * B_HEAD_DIM:CW + (g + 1) * B_HEAD_DIM].astype(BF16) for g in range(B_KV_HEADS)]
        sn = jnp.concatenate([lax.dot_general(qs[g], kn[g], NT_DIMS, preferred_element_type=F32)
                              for g in range(B_KV_HEADS)], axis=0)
        sn = sn + jnp.concatenate([mbn_ref[...]] * nrep, axis=0) + near_bias(1)
        update(sn, vn)
        out = acc_scr[...] * (1.0 / l_scr[...])
        for g in range(B_KV_HEADS):
            o_ref[g] = out[g * R:(g + 1) * R]


def _dsa_s_attn(page_table, q32, mb, kv_new, bias_s, ck_rows, cv_rows):
    nb, n_pages = page_table.shape
    nj = n_pages // PG
    R = B_GROUP * QR
    PR = PAGE_SIZE * B_KV_HEADS

    def page_spec(p):
        return pl.BlockSpec((PR, B_HEAD_DIM), lambda b, j, pt: (pt[b, j * PG + p], 0))

    grid_spec = pltpu.PrefetchScalarGridSpec(
        num_scalar_prefetch=1, grid=(nb, nj),
        in_specs=[pl.BlockSpec((None, B_KV_HEADS, R, B_HEAD_DIM), lambda b, j, pt: (b, 0, 0, 0)),
                  pl.BlockSpec((None, QR, PG * PAGE_SIZE), lambda b, j, pt: (b, 0, j)),
                  pl.BlockSpec((None, QR, PAGE_SIZE), lambda b, j, pt: (b, 0, n_pages)),
                  pl.BlockSpec((None, PAGE_SIZE, 2 * B_KV_HEADS * B_HEAD_DIM), lambda b, j, pt: (b, 0, 0)),
                  pl.BlockSpec(bias_s.shape, lambda b, j, pt: (0, 0, 0, 0))]
                 + [page_spec(p) for p in range(PG)] + [page_spec(p) for p in range(PG)],
        out_specs=pl.BlockSpec((None, B_KV_HEADS, R, B_HEAD_DIM), lambda b, j, pt: (b, 0, 0, 0)),
        scratch_shapes=[pltpu.VMEM((B_KV_HEADS * R, 1), F32), pltpu.VMEM((B_KV_HEADS * R, 1), F32),
                        pltpu.VMEM((B_KV_HEADS * R, B_HEAD_DIM), F32)])
    return pl.pallas_call(
        _dsa_s_attn_body, grid_spec=grid_spec,
        out_shape=jax.ShapeDtypeStruct((nb, B_KV_HEADS, R, B_HEAD_DIM), F32),
        compiler_params=_cparams(("parallel", "arbitrary")), name="dsa_sample_attn",
    )(page_table, q32, mb, mb, kv_new, bias_s, *([ck_rows] * PG), *([cv_rows] * PG))


def _layer_norm(r, g, b):
    mu = jnp.mean(r, axis=-1, keepdims=True)
    d = r - mu
    var = jnp.mean(d * d, axis=-1, keepdims=True)
    return d * lax.rsqrt(var + LN_EPS) * g + b


def _merge_body(oa_ref, ob_ref, gates_ref, x_ref, g1_ref, sh2_ref, sc2_ref, w_ref, lng_ref, lnb_ref,
                x1_ref, h2_ref, *, alpha):
    D = oa_ref.shape[1]
    merged = _sigmoid(gates_ref[:, 0:D]) * oa_ref[...] + _sigmoid(gates_ref[:, D:2 * D]) * ob_ref[...]
    y = jnp.dot(merged.astype(BF16), w_ref[...], preferred_element_type=F32)
    x1 = _layer_norm(alpha * x_ref[...] + g1_ref[...] * y, lng_ref[...], lnb_ref[...])
    x1_ref[...] = x1
    h2_ref[...] = (x1 * (1.0 + sc2_ref[...]) + sh2_ref[...]).astype(BF16)


def _merge(oa, ob, z, x2d, mod, rows_per_mod, w_out_bf, ln_g, ln_b, alpha, tm):
    N, D = x2d.shape
    row = lambda i: (i, 0)
    const = lambda i: (0, 0)
    return pl.pallas_call(
        functools.partial(_merge_body, alpha=alpha), grid=(N // tm,),
        in_specs=[pl.BlockSpec((tm, D), row), pl.BlockSpec((tm, D), row),
                  pl.BlockSpec((tm, 2 * D), lambda i: (i, Z_GATES // (2 * D))),
                  pl.BlockSpec((tm, D), row),
                  _mod_spec(mod, 2, tm, rows_per_mod, D, 1),
                  _mod_spec(mod, 3, tm, rows_per_mod, D, 1),
                  _mod_spec(mod, 4, tm, rows_per_mod, D, 1),
                  pl.BlockSpec((D, D), const), pl.BlockSpec((1, D), const), pl.BlockSpec((1, D), const)],
        out_specs=[pl.BlockSpec((tm, D), row), pl.BlockSpec((tm, D), row)],
        out_shape=[jax.ShapeDtypeStruct((N, D), F32), jax.ShapeDtypeStruct((N, D), BF16)],
        compiler_params=_cparams(("parallel",)), name="merge_out",
    )(oa, ob, z, x2d, mod, mod, mod, w_out_bf, ln_g, ln_b)


def _top_values(x, n, with_rank=False):
    out = []
    rank = jnp.full(x.shape, float(n), F32) if with_rank else None
    for r in range(n):
        m = jnp.max(x, axis=0, keepdims=True)
        out.append(m)
        if with_rank:
            rank = jnp.where(x == m, float(r), rank)
        if r < n - 1:
            x = jnp.where(x == m, -jnp.inf, x)
    return (out, rank) if with_rank else out


_PEER_PAIRS = [(a, b) for a in range(PEER_TOPK) for b in range(PEER_TOPK) if (a + 1) * (b + 1) <= PEER_TOPK]


def _peer_body(h2_ref, x1_ref, g2_ref, wq_ref, sk_ref, u0_ref, un_ref, vt_ref, lng_ref, lnb_ref, o_ref,
               np_scr, r2_scr, e1_scr, e2_scr, acc_scr, act_ab, w_ab, *, tn, ec, nchunks, alpha):
    e = pl.program_id(1)
    ne = pl.num_programs(1) - 1
    n_i1 = ec // N_KEYS

    @pl.when(e == 0)
    def _select():
        q = jnp.dot(h2_ref[...], wq_ref[...], preferred_element_type=F32)
        for h in range(PEER_HEADS):
            sc = []
            for c in range(2):
                qhc = q[:, (2 * h + c) * N_KEYS:(2 * h + c + 1) * N_KEYS].astype(BF16)
                sc.append(lax.dot_general(sk_ref[c], qhc, NT_DIMS, preferred_element_type=F32))
            s1, s2 = sc
            t1 = _top_values(s1, PEER_TOPK)
            t2, rank2 = _top_values(s2, PEER_TOPK, with_rank=True)
            cand = jnp.concatenate([t1[a] + t2[b] for a, b in _PEER_PAIRS], axis=0)
            thr = _top_values(cand, PEER_TOPK)[-1]
            mx = t1[0] + t2[0]
            zsum = jnp.sum(jnp.where(cand >= thr, jnp.exp(cand - mx), 0.0), axis=0, keepdims=True)
            npass = jnp.zeros(s1.shape, F32)
            for r in range(PEER_TOPK):
                npass = npass + jnp.where(s1 + t2[r] >= thr, 1.0, 0.0)
            np_scr[h] = npass
            r2_scr[h] = rank2.astype(BF16)
            e1_scr[h] = jnp.exp(s1 - t1[0]) * (1.0 / zsum)
            e2_scr[h] = jnp.exp(s2 - t2[0]).astype(BF16)
        act_ab[0] = lax.dot_general(u0_ref[...], h2_ref[...], NT_DIMS, preferred_element_type=F32)
        w_ab[1] = jnp.zeros(w_ab.shape[1:], BF16)
        acc_scr[...] = jnp.zeros(acc_scr.shape, F32)

    PK = 16

    def chunk(act_cur, act_nxt, w_cur, w_prev):
        acc_scr[...] += jnp.dot(vt_ref[...], w_prev[...], preferred_element_type=F32)

        act_nxt[...] = lax.dot_general(un_ref[...], h2_ref[...], NT_DIMS, preferred_element_type=F32)
        for jj in range(n_i1):
            i1 = e * n_i1 + jj
            a = act_cur[jj * N_KEYS:(jj + 1) * N_KEYS, :]
            gsum = jnp.zeros((N_KEYS // PK, PK, tn), BF16)
            for h in range(PEER_HEADS):
                npass_row = jnp.broadcast_to(np_scr[h, pl.ds(i1, 1), :], (PK, tn)).astype(BF16)
                e1_row = jnp.broadcast_to(e1_scr[h, pl.ds(i1, 1), :], (PK, tn)).astype(BF16)
                sel = r2_scr[h].reshape(N_KEYS // PK, PK, tn) < npass_row
                e2 = e2_scr[h].reshape(N_KEYS // PK, PK, tn)
                gsum = gsum + jnp.where(sel, e2, jnp.zeros_like(e2)) * e1_row
            gelu = 0.5 * a * (1.0 + lax.erf(a * (2.0 ** -0.5)))
            w_cur[jj * N_KEYS:(jj + 1) * N_KEYS, :] = gsum.reshape(N_KEYS, tn) * gelu.astype(BF16)

    @pl.when(e < ne)
    def _chunk():
        slot = e % 2
        chunk(act_ab.at[slot], act_ab.at[1 - slot], w_ab.at[slot], w_ab.at[1 - slot])

    @pl.when(e == ne)
    def _fin():
        acc = acc_scr[...] + jnp.dot(vt_ref[...], w_ab[(nchunks - 1) % 2], preferred_element_type=F32)
        y = acc.T
        o_ref[...] = _layer_norm(alpha * x1_ref[...] + g2_ref[...] * y, lng_ref[...], lnb_ref[...])


def _peer(h2, x1, mod, rows_per_mod, wq_bf, sk_bf, u_bf, vt_bf, ln_g, ln_b, alpha, tn, ec):
    N, D = x1.shape
    E = u_bf.shape[0]
    ne = E // ec
    row = lambda i, e: (i, 0)
    const2 = lambda i, e: (0, 0)
    if mod.ndim == 3:
        g2_spec = pl.BlockSpec((None, 1, D), lambda i, e: ((i * tn) // rows_per_mod, 0, 5))
    else:
        g2_spec = pl.BlockSpec((tn, D), lambda i, e: (i, 5))
    return pl.pallas_call(
        functools.partial(_peer_body, tn=tn, ec=ec, nchunks=ne, alpha=alpha), grid=(N // tn, ne + 1),
        in_specs=[pl.BlockSpec((tn, D), row), pl.BlockSpec((tn, D), row), g2_spec,
                  _resident(wq_bf.shape, const2),
                  _resident(sk_bf.shape, lambda i, e: (0, 0, 0)),
                  _resident((ec, D), const2),
                  pl.BlockSpec((ec, D), lambda i, e: (jnp.minimum(e + 1, ne - 1), 0)),
                  pl.BlockSpec((D, ec), lambda i, e: (0, jnp.maximum(e - 1, 0))),
                  pl.BlockSpec((1, D), const2), pl.BlockSpec((1, D), const2)],
        out_specs=pl.BlockSpec((tn, D), row),
        out_shape=jax.ShapeDtypeStruct((N, D), F32),
        scratch_shapes=[pltpu.VMEM((PEER_HEADS, N_KEYS, tn), F32), pltpu.VMEM((PEER_HEADS, N_KEYS, tn), BF16),
                        pltpu.VMEM((PEER_HEADS, N_KEYS, tn), F32), pltpu.VMEM((PEER_HEADS, N_KEYS, tn), BF16),
                        pltpu.VMEM((D, tn), F32),
                        pltpu.VMEM((2, ec, tn), F32), pltpu.VMEM((2, ec, tn), BF16)],
        compiler_params=_cparams(("parallel", "arbitrary")), name="peer",
    )(h2, x1, mod, wq_bf, sk_bf, u_bf, u_bf, vt_bf, ln_g, ln_b)


def _pack_w_in(w_in):
    D = w_in.shape[0]
    idx_w = w_in[:, 5632:5956]
    pad = jnp.zeros((D, Z_GATES - Z_QI - idx_w.shape[1]), w_in.dtype)
    return jnp.concatenate([w_in[:, :5632], idx_w, pad, w_in[:, 5956:]], axis=1).astype(BF16)


def _prompt_buckets(tq):
    i = np.arange(tq)[:, None]
    j = np.arange(tq)[None, :]
    return np.stack([_t5_bucket_np(np.maximum(i - j, 0)), _t5_bucket_np(tq + i - j)])


def _sample_buckets():
    t = np.arange(QR)[:, None]
    o = np.arange(PAGE_SIZE)[None, :]
    return np.stack([_t5_bucket_np(PAGE_SIZE + t - o), _t5_bucket_np(np.maximum(t - o, 0))])


def _layer_tail(oa, ob, z, x2d, mod, rows_per_mod, wts, alpha, tm, tn_peer):
    x1, h2 = _merge(oa, ob, z, x2d, mod, rows_per_mod, wts["w_out"], wts["ln1_g"], wts["ln1_b"], alpha, tm)
    return _peer(h2, x1, mod, rows_per_mod, wts["w_peer_q"], wts["sub_keys"], wts["peer_u"], wts["peer_vt"],
                 wts["ln2_g"], wts["ln2_b"], alpha, tn_peer, 1024)


def kernel(x_prompt, x_sample, cache_k, cache_v, cache_kidx, state_hgrn, page_table, c_prompt, c_sample,
           w_ada, b_ada, w_in, lb_param, a_gnorm, t5_table, w_out, ln1_g, ln1_b, w_peer_q, peer_sub_keys,
           peer_u, peer_v, ln2_g, ln2_b):
    depth = w_ada.shape[0]
    assert depth == 1 and lb_param.shape[0] == 2
    alpha = (2.0 * depth) ** 0.25
    B, T, D = x_prompt.shape
    NB, TS, _ = x_sample.shape
    n_pages = page_table.shape[1]
    past_len = n_pages * PAGE_SIZE

    wts = dict(
        w_out=w_out[0].astype(BF16), ln1_g=ln1_g, ln1_b=ln1_b, ln2_g=ln2_g, ln2_b=ln2_b,
        w_peer_q=w_peer_q[0].astype(BF16), sub_keys=peer_sub_keys[0].astype(BF16),
        peer_u=peer_u[0].astype(BF16), peer_vt=peer_v[0].T.astype(BF16))
    w_in_bf = _pack_w_in(w_in[0])
    gnorm = a_gnorm.reshape(1, A_DK)

    mod = _ada(jnp.concatenate([c_prompt, c_sample], axis=0), w_ada[0].astype(BF16), b_ada)
    mod_p = mod[:B].reshape(B, 1, 6 * D)
    mod_s = jnp.repeat(mod[B:], TS, axis=0)

    xp = x_prompt.reshape(B * T, D)
    zp = _inproj(xp, mod_p, T, w_in_bf, 256, 1024)
    oa_p, s_p = _hgrn(zp, jnp.zeros((B, A_HEADS, A_DK, A_DK), F32), lb_param, gnorm, B, T, 256, A_CHUNK, 256, 4)
    tq = 256
    bias_p = _bias_tiles(t5_table, _prompt_buckets(tq))
    bias_g = bias_p.reshape(B_KV_HEADS, B_GROUP, 2, tq, tq).transpose(0, 2, 1, 3, 4)
    bias_g = bias_g.reshape(B_KV_HEADS, 2, B_GROUP * tq, tq)
    bias_g = jnp.concatenate([bias_g, jnp.zeros((B_KV_HEADS, 1, B_GROUP * tq, tq), F32)], axis=1)
    ob_p = _dsa_prompt(zp, bias_g, B, T, tq)
    y_p = _layer_tail(oa_p, ob_p, zp, xp, mod_p, T, wts, alpha, 256, 512)

    xs = x_sample.reshape(NB * TS, D)
    zs = _inproj(xs, mod_s, 1, w_in_bf, min(256, NB * TS), 1024)
    zs3 = zs.reshape(NB, TS, Z_WIDTH)
    zs_pad = jnp.pad(zs3, ((0, 0), (0, QR - TS), (0, 0))).reshape(NB * QR, Z_WIDTH)
    assert TS <= QR
    oa_s8, s_s = _hgrn(zs_pad, state_hgrn[0], lb_param, gnorm, NB, QR, QR, QR, TS, A_HEADS)
    oa_s = oa_s8.reshape(NB, QR, D)[:, :TS].reshape(NB * TS, D)

    qi_s = zs3[:, :, Z_QI:Z_QI + IDX_HEADS * IDX_DIM].reshape(NB, TS, IDX_HEADS, IDX_DIM)
    qi32 = jnp.pad(qi_s.transpose(0, 2, 1, 3), ((0, 0), (0, 0), (0, QR - TS), (0, 0)))
    qi32 = qi32.reshape(NB, IDX_HEADS * QR, IDX_DIM).astype(BF16)
    wi_s = zs3[:, :, Z_WI:Z_WI + IDX_HEADS].transpose(0, 2, 1)
    w32 = jnp.pad(wi_s, ((0, 0), (0, 0), (0, QR - TS))).reshape(NB, IDX_HEADS * QR, 1)
    w32 = jnp.broadcast_to(w32, (NB, IDX_HEADS * QR, 128))
    ki_new = jnp.pad(zs3[:, :, Z_KI:Z_KI + IDX_DIM], ((0, 0), (0, PAGE_SIZE - TS), (0, 0))).astype(BF16)
    ki_new_t = jnp.swapaxes(ki_new, 1, 2)
    kv_new = jnp.pad(zs3[:, :, Z_KB:Z_KB + 512], ((0, 0), (0, PAGE_SIZE - TS), (0, 0)))
    q_s = zs3[:, :, Z_QB:Z_QB + 1024].reshape(NB, TS, B_KV_HEADS, B_GROUP, B_HEAD_DIM)
    q32 = jnp.pad(q_s.transpose(0, 2, 3, 1, 4), ((0, 0), (0, 0), (0, 0), (0, QR - TS), (0, 0)))
    q32 = q32.reshape(NB, B_KV_HEADS, B_GROUP * QR, B_HEAD_DIM)

    ksel = min(TOPK_MAX, (past_len + TS) // 4)
    kidx_t = jnp.swapaxes(cache_kidx[0], 1, 2)
    mb_s = _dsa_s_select(page_table, qi32, w32, ki_new_t, kidx_t, ksel, TS)
    bias_s = _bias_tiles(t5_table, _sample_buckets())
    n_pool = cache_k.shape[1]
    ck = cache_k.reshape(n_pool * PAGE_SIZE * B_KV_HEADS, B_HEAD_DIM)
    cv = cache_v.reshape(n_pool * PAGE_SIZE * B_KV_HEADS, B_HEAD_DIM)
    o32 = _dsa_s_attn(page_table, q32, mb_s, kv_new, bias_s, ck, cv)
    ob_s = o32.reshape(NB, B_KV_HEADS, B_GROUP, QR, B_HEAD_DIM)[:, :, :, :TS]
    ob_s = ob_s.transpose(0, 3, 1, 2, 4).reshape(NB * TS, D)
    y_s = _layer_tail(oa_s, ob_s, zs, xs, mod_s, 1, wts, alpha, NB * TS, NB * TS)

    def kv_out(z, nb, t):
        k = z[:, Z_KB:Z_KB + 256].reshape(1, nb, t, B_KV_HEADS, B_HEAD_DIM)
        v = z[:, Z_VB:Z_VB + 256].reshape(1, nb, t, B_KV_HEADS, B_HEAD_DIM)
        ki = z[:, Z_KI:Z_KI + IDX_DIM].reshape(1, nb, t, IDX_DIM)
        return k, v, ki

    kp, vp, ip = kv_out(zp, B, T)
    ks, vs, is_ = kv_out(zs, NB, TS)
    return (y_p.reshape(B, T, D), y_s.reshape(NB, TS, D), kp, vp, ip, s_p[None],
            ks, vs, is_, s_s[None])
```

```python
import functools
import math

import numpy as np
import jax
import jax.numpy as jnp
from jax import lax
from jax.experimental import pallas as pl
from jax.experimental.pallas import tpu as pltpu

F32 = jnp.float32
BF16 = jnp.bfloat16
I32 = jnp.int32

A_HEADS = 8
A_DK = 128
A_CHUNK = 32
B_HEADS = 8
B_KV_HEADS = 2
B_GROUP = 4
B_HEAD_DIM = 128
IDX_HEADS = 4
IDX_DIM = 64
TOPK_MAX = 256
N_BUCKETS = 32
MAX_DISTANCE = 128
PEER_HEADS = 8
N_KEYS = 128
PEER_TOPK = 16
PAGE_SIZE = 128
LN_EPS = 1e-5
RMS_EPS = 1e-6

Z_A = 0
Z_QB = 4096
Z_KB = 5120
Z_VB = 5376
Z_QI = 5632
Z_KI = 5888
Z_WI = 5952
Z_GATES = 6144
Z_WIDTH = 8192

NEG = -1e30
INT_MIN = -2 ** 31
LOG2E = 1.4426950408889634
NT_DIMS = (((1,), (1,)), ((), ()))
VMEM_LIMIT = 56 * 1024 * 1024


def _cparams(sem):
    return pltpu.CompilerParams(dimension_semantics=sem, vmem_limit_bytes=VMEM_LIMIT)


def _resident(shape, index_map):
    return pl.BlockSpec(shape, index_map, pipeline_mode=pl.Buffered(1))


def _sigmoid(x):
    return jax.nn.sigmoid(x)


def _ada_body(c_ref, w_ref, b_ref, o_ref):
    c = c_ref[...]
    s = c * _sigmoid(c)
    o_ref[...] = jnp.dot(s.astype(BF16), w_ref[...], preferred_element_type=F32) + b_ref[...]


def _ada(c, w_bf, b):
    R, D = c.shape
    N = w_bf.shape[1]
    tn = N // 4
    return pl.pallas_call(
        _ada_body, grid=(N // tn,),
        in_specs=[pl.BlockSpec((R, D), lambda j: (0, 0)),
                  pl.BlockSpec((D, tn), lambda j: (0, j)),
                  pl.BlockSpec((1, tn), lambda j: (0, j))],
        out_specs=pl.BlockSpec((R, tn), lambda j: (0, j)),
        out_shape=jax.ShapeDtypeStruct((R, N), F32),
        compiler_params=_cparams(("parallel",)), name="ada")(c, w_bf, b)


def _mod_spec(mod, blk, tm, rows_per_mod, D, nidx):
    if mod.ndim == 3:
        if nidx == 1:
            return pl.BlockSpec((None, 1, D), lambda i: ((i * tm) // rows_per_mod, 0, blk))
        return pl.BlockSpec((None, 1, D), lambda i, j: ((i * tm) // rows_per_mod, 0, blk))
    if nidx == 1:
        return pl.BlockSpec((tm, D), lambda i: (i, blk))
    return pl.BlockSpec((tm, D), lambda i, j: (i, blk))


def _inproj_body(x_ref, sh_ref, sc_ref, w_ref, o_ref, k_ref, v_ref, *, tn):
    tm = x_ref.shape[0]
    h = (x_ref[...] * (1.0 + sc_ref[...]) + sh_ref[...]).astype(BF16)
    for j in range(o_ref.shape[1] // tn):
        zt = jnp.dot(h, w_ref[:, j * tn:(j + 1) * tn], preferred_element_type=F32)
        o_ref[:, j * tn:(j + 1) * tn] = zt
        if j * tn <= Z_KB and Z_VB + B_KV_HEADS * B_HEAD_DIM <= (j + 1) * tn:
            for g in range(B_KV_HEADS):
                ko = Z_KB - j * tn + g * B_HEAD_DIM
                vo = Z_VB - j * tn + g * B_HEAD_DIM
                k_ref[pl.ds(g, tm, stride=B_KV_HEADS), :] = zt[:, ko:ko + B_HEAD_DIM]
                v_ref[pl.ds(g, tm, stride=B_KV_HEADS), :] = zt[:, vo:vo + B_HEAD_DIM]


def _inproj(x2d, mod, rows_per_mod, w_bf, tm, tn):
    N, D = x2d.shape
    W = w_bf.shape[1]
    kv_spec = pl.BlockSpec((tm * B_KV_HEADS, B_HEAD_DIM), lambda i: (i, 0))
    kv_shape = jax.ShapeDtypeStruct((N * B_KV_HEADS, B_HEAD_DIM), F32)
    return pl.pallas_call(
        functools.partial(_inproj_body, tn=tn), grid=(N // tm,),
        in_specs=[pl.BlockSpec((tm, D), lambda i: (i, 0)),
                  _mod_spec(mod, 0, tm, rows_per_mod, D, 1),
                  _mod_spec(mod, 1, tm, rows_per_mod, D, 1),
                  _resident((D, W), lambda i: (0, 0))],
        out_specs=[pl.BlockSpec((tm, W), lambda i: (i, 0)), kv_spec, kv_spec],
        out_shape=[jax.ShapeDtypeStruct((N, W), F32), kv_shape, kv_shape],
        compiler_params=_cparams(("parallel",)), name="inproj")(x2d, mod, mod, w_bf)


def _hgrn_body(q_ref, f_ref, i_ref, g_ref, lbp_ref, gn_ref, s0_ref, o_ref, sout_ref, st_scr,
               *, C, Tb, t_valid, HB):
    tb = pl.program_id(2)
    shift = int(math.log2(C))
    nchunk = Tb // C
    r = lax.broadcasted_iota(I32, (Tb, Tb), 0)
    c = lax.broadcasted_iota(I32, (Tb, Tb), 1)
    tril = (jnp.right_shift(r, shift) == jnp.right_shift(c, shift)) & (c <= r)
    trilf = jnp.where(tril, 1.0, 0.0)
    Tp = max(Tb, 128)
    colchunk = jnp.right_shift(lax.broadcasted_iota(I32, (A_DK, Tp), 1), shift)

    lbp = lbp_ref[...]
    mx = jnp.max(lbp, axis=0, keepdims=True)
    e = jnp.exp(lbp - mx)
    lb_all = e[0:1, :] / jnp.sum(e, axis=0, keepdims=True)

    @pl.when(tb == 0)
    def _init():
        for hh in range(HB):
            st_scr[hh] = s0_ref[hh].T

    W = HB * A_DK
    f = lb_all + (1.0 - lb_all) * _sigmoid(f_ref[...])
    lf = jnp.log(f)
    k = 1.0 - f
    if t_valid < Tb:
        ok = lax.broadcasted_iota(I32, (Tb, W), 0) < t_valid
        lf = jnp.where(ok, lf, 0.0)
        k = jnp.where(ok, k, 0.0)
    bcum = jnp.dot(trilf, lf, precision=lax.Precision.HIGHEST,
                   preferred_element_type=F32)
    tot = jnp.concatenate(
        [jnp.broadcast_to(bcum[(j + 1) * C - 1:(j + 1) * C, :], (C, W)) for j in range(nchunk)], axis=0)
    qa = q_ref[...]
    qe_b = ((qa * _sigmoid(qa)) * jnp.exp(bcum)).astype(BF16)
    ke_b = (k * jnp.exp(-bcum)).astype(BF16)
    kd = k * jnp.exp(tot - bcum)
    v = i_ref[...]
    v_b = v.astype(BF16)
    if Tb < 128:
        zpad = jnp.zeros((128 - Tb, W), F32)
        v_p = jnp.concatenate([v, zpad], axis=0)
        kd_p = jnp.concatenate([kd, zpad], axis=0)
    else:
        v_p, kd_p = v, kd
    kd_b = kd_p.astype(BF16)
    heads = [(hh * A_DK, (hh + 1) * A_DK) for hh in range(HB)]

    scores = [jnp.where(tril, lax.dot_general(qe_b[:, lo:hi], ke_b[:, lo:hi], NT_DIMS,
                                              preferred_element_type=F32), 0.0).astype(BF16) for lo, hi in heads]
    o = [jnp.dot(scores[hh], v_b[:, lo:hi], preferred_element_type=F32) for hh, (lo, hi) in enumerate(heads)]
    vT = [v_p[:, lo:hi].T for lo, hi in heads]
    ST = [st_scr[hh] for hh in range(HB)]
    o_inter = [[] for _ in range(HB)]
    for j in range(nchunk):
        for hh, (lo, hi) in enumerate(heads):
            o_inter[hh].append(lax.dot_general(qe_b[j * C:(j + 1) * C, lo:hi], ST[hh].astype(BF16), NT_DIMS,
                                               preferred_element_type=F32))
            d_c = jnp.exp(tot[j * C:j * C + 1, lo:hi])
            vTm = jnp.where(colchunk == j, vT[hh], 0.0) if nchunk > 1 else vT[hh]
            ST[hh] = ST[hh] * d_c + jnp.dot(vTm.astype(BF16), kd_b[:, lo:hi], preferred_element_type=F32)
    gn = gn_ref[...]
    ga = g_ref[...]
    gate = ga * _sigmoid(ga)
    for hh, (lo, hi) in enumerate(heads):
        st_scr[hh] = ST[hh]
        oh = o[hh] + jnp.concatenate(o_inter[hh], axis=0)
        ms = jnp.mean(oh * oh, axis=-1, keepdims=True)
        o_ref[:, lo:hi] = oh * lax.rsqrt(ms + RMS_EPS) * gn * gate[:, lo:hi]

    @pl.when(tb == pl.num_programs(2) - 1)
    def _fin():
        for hh in range(HB):
            sout_ref[hh] = st_scr[hh].T


def _hgrn(z, s0, lb_param, gnorm, B, T, Tb, C, t_valid, HB):
    nt = T // Tb
    H = A_HEADS
    W = HB * A_DK
    ng = H // HB

    def col(group):
        return pl.BlockSpec((Tb, W), lambda b, h, t: (b * nt + t, group * ng + h))

    return pl.pallas_call(
        functools.partial(_hgrn_body, C=C, Tb=Tb, t_valid=t_valid, HB=HB),
        grid=(B, ng, nt),
        in_specs=[col(0), col(1), col(2), col(3),
                  pl.BlockSpec((2, W), lambda b, h, t: (0, h)),
                  pl.BlockSpec((1, A_DK), lambda b, h, t: (0, 0)),
                  pl.BlockSpec((None, HB, A_DK, A_DK), lambda b, h, t: (b, h, 0, 0))],
        out_specs=[pl.BlockSpec((Tb, W), lambda b, h, t: (b * nt + t, h)),
                   pl.BlockSpec((None, HB, A_DK, A_DK), lambda b, h, t: (b, h, 0, 0))],
        out_shape=[jax.ShapeDtypeStruct((B * T, H * A_DK), F32),
                   jax.ShapeDtypeStruct((B, H, A_DK, A_DK), F32)],
        scratch_shapes=[pltpu.VMEM((HB, A_DK, A_DK), F32)],
        compiler_params=_cparams(("parallel", "parallel", "arbitrary")), name="hgrn",
    )(z, z, z, z, lb_param, gnorm, s0)


def _t5_bucket_np(rel):
    rel = np.asarray(rel, np.int64)
    max_exact = N_BUCKETS // 2
    relf = np.maximum(rel, 1).astype(np.float64)
    large = max_exact + (np.log(relf / max_exact) / math.log(MAX_DISTANCE / max_exact)
                         * (N_BUCKETS - max_exact)).astype(np.int64)
    large = np.minimum(large, N_BUCKETS - 1)
    return np.where(rel < max_exact, rel, large).astype(np.int32)


def _bias_body(tab_ref, bkt_ref, o_ref):
    h = pl.program_id(0)
    b = bkt_ref[...]
    acc = jnp.zeros(b.shape, F32)
    for i in range(N_BUCKETS):
        acc = jnp.where(b == i, tab_ref[i, h], acc)
    o_ref[...] = (acc - tab_ref[N_BUCKETS - 1, h]) * LOG2E


def _bias_tiles(t5_table, buckets):
    K, R, Cc = buckets.shape
    return pl.pallas_call(
        _bias_body, grid=(B_HEADS, K),
        in_specs=[pl.BlockSpec(memory_space=pltpu.SMEM),
                  pl.BlockSpec((None, R, Cc), lambda h, k: (k, 0, 0))],
        out_specs=pl.BlockSpec((None, None, R, Cc), lambda h, k: (h, k, 0, 0)),
        out_shape=jax.ShapeDtypeStruct((B_HEADS, K, R, Cc), F32),
        compiler_params=_cparams(("parallel", "parallel")), name="t5_bias")(t5_table, jnp.asarray(buckets))


def _ordered_key(score):
    score = jnp.where(score == 0.0, 0.0, score)
    bits = pltpu.bitcast(score, I32)
    return jnp.where(bits < 0, jnp.bitwise_xor(bits, jnp.int32(0x7FFFFFFF)), bits)


def _bcast_lanes(col, n):
    return jnp.broadcast_to(col, (col.shape[0], n))


def _dsa_prompt_body(q_ref, kv_ref, idxq_ref, ki_ref, bias_ref, o_ref,
                     key_scr, mb_scr, m_scr, l_scr, acc_scr, s_scr, *, tq, ksel):
    qi = pl.program_id(1)
    nvalid = qi + 1
    npair = (nvalid + 1) // 2
    tw = 2 * tq

    idxq = idxq_ref[...]
    q_idx = [idxq[:, h * IDX_DIM:(h + 1) * IDX_DIM].astype(BF16) for h in range(IDX_HEADS)]
    w_t = idxq[:, Z_KI - Z_QI:Z_KI - Z_QI + 128].T
    w_rows = [w_t[Z_WI - Z_KI + h:Z_WI - Z_KI + h + 1, :] for h in range(IDX_HEADS)]
    key_i = lax.broadcasted_iota(I32, (tq, tq), 0)
    qry_t = qi * tq + lax.broadcasted_iota(I32, (tq, tq), 1)

    def idx_chunk(c, carry):
        off = pl.multiple_of(c * tq, tq)
        kic = ki_ref[pl.ds(off, tq), 0:IDX_DIM].astype(BF16)
        acc = jnp.zeros((tq, tq), F32)
        for h in range(IDX_HEADS):
            s = lax.dot_general(kic, q_idx[h], NT_DIMS, preferred_element_type=F32)
            acc = acc + jnp.maximum(s * (IDX_DIM ** -0.5), 0.0) * w_rows[h]
        score = acc * (IDX_HEADS ** -0.5)
        score = jnp.where(off + key_i <= qry_t, score, -jnp.inf)
        key_scr[pl.ds(off, tq), :] = _ordered_key(score)
        return carry

    lax.fori_loop(0, nvalid, idx_chunk, 0)

    @pl.when(nvalid % 2 == 1)
    def _pad():
        off = pl.multiple_of(nvalid * tq, tq)
        key_scr[pl.ds(off, tq), :] = jnp.full((tq, tq), INT_MIN, I32)
        mb_scr[:, pl.ds(off, tq)] = jnp.full((tq, tq), NEG, F32)

    NACC = 8

    def count(pred_fn):
        def body(c, cnt):
            off = pl.multiple_of(c * tw, tw)
            x = key_scr[pl.ds(off, tw), :].reshape(tw // (8 * NACC), NACC, 8, tq)
            return cnt + jnp.sum(jnp.where(pred_fn(x), 1.0, 0.0), axis=0)
        cnt = lax.fori_loop(0, npair, body, jnp.zeros((NACC, 8, tq), F32))
        return jnp.sum(jnp.sum(cnt, axis=0), axis=0, keepdims=True)

    def bit_step(it, thr):
        cand = thr + jnp.left_shift(jnp.int32(1), 31 - it)
        return jnp.where(count(lambda x: x >= cand) >= ksel, cand, thr)

    thr = lax.fori_loop(0, 32, bit_step, jnp.full((1, tq), INT_MIN, I32))
    need = ksel - count(lambda x: x > thr)
    tril = jnp.where(lax.broadcasted_iota(I32, (tq, tq), 1) < key_i, 1.0, 0.0).astype(BF16)

    def mask_chunk(c, carry):
        off = pl.multiple_of(c * tq, tq)
        x = key_scr[pl.ds(off, tq), :]
        eq = x == thr
        eqf = jnp.where(eq, 1.0, 0.0)
        rank = jnp.dot(tril, eqf.astype(BF16), preferred_element_type=F32) + carry
        mb = jnp.where(x > thr, 0.0, jnp.where(eq, jnp.where(rank < need, 0.0, NEG), NEG))
        mb = jnp.where(off + key_i <= qry_t, mb, NEG)
        mb_scr[:, pl.ds(off, tq)] = mb.T
        return carry + jnp.sum(eqf, axis=0, keepdims=True)

    lax.fori_loop(0, nvalid, mask_chunk, jnp.zeros((1, tq), F32))

    scale = (B_HEAD_DIM ** -0.5) * LOG2E
    nsl = tw // 128

    def bias_kind(c):
        d = qi - c
        return jnp.where(d == 0, 0, jnp.where(d == 1, 1, 2))

    for g in range(B_KV_HEADS):
        qg = jnp.concatenate(
            [q_ref[:, (g * B_GROUP + r) * B_HEAD_DIM:(g * B_GROUP + r + 1) * B_HEAD_DIM] for r in range(B_GROUP)],
            axis=0)
        qg = (qg * scale).astype(BF16)

        def logits(c2, g=g, qg=qg):
            off = pl.multiple_of(c2 * tw, tw)
            kc = kv_ref[pl.ds(off, tw), g * B_HEAD_DIM:(g + 1) * B_HEAD_DIM].astype(BF16)
            s = lax.dot_general(qg, kc, NT_DIMS, preferred_element_type=F32)
            mb = mb_scr[:, pl.ds(off, tw)]
            bias = jnp.concatenate([bias_ref[g, bias_kind(2 * c2)], bias_ref[g, bias_kind(2 * c2 + 1)]], axis=1)
            return jnp.concatenate([s[r * tq:(r + 1) * tq] + mb for r in range(B_GROUP)], axis=0) + bias

        def max_body(c2, m):
            s = logits(c2)
            s_scr[:, pl.ds(pl.multiple_of(c2 * tw, tw), tw)] = s
            for k in range(nsl):
                m = jnp.maximum(m, s[:, k * 128:(k + 1) * 128])
            return m

        m_scr[...] = lax.fori_loop(0, npair, max_body, jnp.full(m_scr.shape, NEG, F32))
        m_row = _bcast_lanes(jnp.max(m_scr[...], axis=1, keepdims=True), 128)
        m_scr[...] = m_row
        l_scr[...] = jnp.zeros(l_scr.shape, F32)
        acc_scr[...] = jnp.zeros(acc_scr.shape, F32)

        def sum_body(c2, carry, g=g):
            off = pl.multiple_of(c2 * tw, tw)
            vc = kv_ref[pl.ds(off, tw), (B_KV_HEADS + g) * B_HEAD_DIM:(B_KV_HEADS + g + 1) * B_HEAD_DIM].astype(BF16)
            m = m_scr[...]
            p = jnp.exp2(s_scr[:, pl.ds(off, tw)] - jnp.concatenate([m] * nsl, axis=1))
            l = l_scr[...]
            for k in range(nsl):
                l = l + p[:, k * 128:(k + 1) * 128]
            l_scr[...] = l
            acc_scr[...] += jnp.dot(p.astype(BF16), vc, preferred_element_type=F32)
            return carry

        lax.fori_loop(0, npair, sum_body, 0)
        out = acc_scr[...] * (1.0 / jnp.sum(l_scr[...], axis=1, keepdims=True))
        for r in range(B_GROUP):
            h = g * B_GROUP + r
            o_ref[:, h * B_HEAD_DIM:(h + 1) * B_HEAD_DIM] = out[r * tq:(r + 1) * tq]


def _dsa_prompt(z, bias_g, B, T, tq):
    nq = T // tq
    ksel = min(TOPK_MAX, T // 4)
    assert tq >= ksel and tq % 128 == 0 and nq % 2 == 0
    R = B_GROUP * tq
    return pl.pallas_call(
        functools.partial(_dsa_prompt_body, tq=tq, ksel=ksel),
        grid=(B, nq),
        in_specs=[pl.BlockSpec((tq, 1024), lambda b, i: (b * nq + i, Z_QB // 1024)),
                  _resident((T, 512), lambda b, i: (b, Z_KB // 512)),
                  pl.BlockSpec((tq, 512), lambda b, i: (b * nq + i, Z_QI // 512)),
                  _resident((T, 128), lambda b, i: (b, Z_KI // 128)),
                  _resident(bias_g.shape, lambda b, i: (0, 0, 0, 0))],
        out_specs=pl.BlockSpec((tq, 1024), lambda b, i: (b * nq + i, 0)),
        out_shape=jax.ShapeDtypeStruct((B * T, 1024), F32),
        scratch_shapes=[pltpu.VMEM((T, tq), I32), pltpu.VMEM((tq, T), F32),
                        pltpu.VMEM((R, 128), F32), pltpu.VMEM((R, 128), F32), pltpu.VMEM((R, B_HEAD_DIM), F32),
                        pltpu.VMEM((R, T), F32)],
        compiler_params=_cparams(("parallel", "arbitrary")), name="dsa_prompt",
    )(z, z, z, z, bias_g)


PG = 8
QR = 8


def _idx_score(qi32, w32, keys_t_bf):
    s = jnp.dot(qi32, keys_t_bf, preferred_element_type=F32)
    s = jnp.maximum(s * (IDX_DIM ** -0.5), 0.0) * w32
    acc = s[0:QR]
    for h in range(1, IDX_HEADS):
        acc = acc + s[h * QR:(h + 1) * QR]
    return acc * (IDX_HEADS ** -0.5)


def _dsa_s_scores_body(pt_ref, qi_ref, w_ref, *rest, pgs):
    pages, key_ref = rest[:pgs], rest[pgs]
    qi32, w32 = qi_ref[...], w_ref[...]
    for p in range(pgs):
        key_ref[:, p * PAGE_SIZE:(p + 1) * PAGE_SIZE] = _ordered_key(
            _idx_score(qi32, w32, pages[p][...].astype(BF16)))


def _dsa_s_scores(page_table, qi32, w32, kidx_t):
    nb, n_pages = page_table.shape
    pgs = math.gcd(n_pages, 16)
    nj = n_pages // pgs

    def page_spec(p):
        return pl.BlockSpec((None, IDX_DIM, PAGE_SIZE), lambda b, j, pt: (pt[b, j * pgs + p], 0, 0))

    grid_spec = pltpu.PrefetchScalarGridSpec(
        num_scalar_prefetch=1, grid=(nb, nj),
        in_specs=[pl.BlockSpec((None, IDX_HEADS * QR, IDX_DIM), lambda b, j, pt: (b, 0, 0)),
                  pl.BlockSpec((None, IDX_HEADS * QR, 128), lambda b, j, pt: (b, 0, 0))]
                 + [page_spec(p) for p in range(pgs)],
        out_specs=pl.BlockSpec((None, QR, pgs * PAGE_SIZE), lambda b, j, pt: (b, 0, j)))
    return pl.pallas_call(
        functools.partial(_dsa_s_scores_body, pgs=pgs), grid_spec=grid_spec,
        out_shape=jax.ShapeDtypeStruct((nb, QR, n_pages * PAGE_SIZE), I32),
        compiler_params=_cparams(("parallel", "parallel")), name="dsa_sample_scores",
    )(page_table, qi32, w32, *([kidx_t] * pgs))


SQ = 8


def _dsa_s_select_body(key_ref, qi_ref, w_ref, kin_ref, mb_ref, *, n_pages, ksel, t_new):
    R = SQ * QR
    row = lax.broadcasted_iota(I32, (QR, PAGE_SIZE), 0)
    col = lax.broadcasted_iota(I32, (QR, PAGE_SIZE), 1)
    new_ok1 = (col <= row) & (col < t_new)
    key_new = jnp.concatenate(
        [_ordered_key(jnp.where(new_ok1, _idx_score(qi_ref[q], w_ref[q], kin_ref[q]), -jnp.inf)) for q in range(SQ)],
        axis=0)
    new_ok = jnp.concatenate([new_ok1] * SQ, axis=0)

    def slab(s):
        if s == n_pages:
            return key_new
        return key_ref[:, :, s * PAGE_SIZE:(s + 1) * PAGE_SIZE].reshape(R, PAGE_SIZE)

    def count(pred_fn):
        cnt = jnp.zeros((R, 128), F32)
        for s in range(n_pages + 1):
            cnt = cnt + jnp.where(pred_fn(slab(s)), 1.0, 0.0)
        return jnp.sum(cnt, axis=1, keepdims=True)

    def bit_step(it, thr):
        cand = thr + jnp.left_shift(jnp.int32(1), 31 - it)
        tot = count(lambda x: x >= cand)
        return jnp.where(_bcast_lanes(tot, 128) >= ksel, cand, thr)

    thr = lax.fori_loop(0, 32, bit_step, jnp.full((R, 128), INT_MIN, I32))
    need = _bcast_lanes(ksel - count(lambda x: x > thr), 128)
    triu = jnp.where(lax.broadcasted_iota(I32, (128, 128), 0) < lax.broadcasted_iota(I32, (128, 128), 1),
                     1.0, 0.0).astype(BF16)
    ones = jnp.ones((128, 128), BF16)
    carry = jnp.zeros((R, 128), F32)
    for s in range(n_pages + 1):
        x = slab(s)
        eq = x == thr
        eqb = jnp.where(eq, 1.0, 0.0).astype(BF16)
        rank = jnp.dot(eqb, triu, preferred_element_type=F32) + carry
        mb = jnp.where(x > thr, 0.0, jnp.where(eq, jnp.where(rank < need, 0.0, NEG), NEG))
        if s == n_pages:
            mb = jnp.where(new_ok, mb, NEG)
        mb_ref[:, :, s * PAGE_SIZE:(s + 1) * PAGE_SIZE] = mb.reshape(SQ, QR, PAGE_SIZE)
        carry = carry + jnp.dot(eqb, ones, preferred_element_type=F32)


def _dsa_s_select(keys, qi32, w32, ki_new_t, ksel, t_new):
    nb, _, L = keys.shape
    n_pages = L // PAGE_SIZE
    Lp = L + PAGE_SIZE
    assert nb % SQ == 0
    return pl.pallas_call(
        functools.partial(_dsa_s_select_body, n_pages=n_pages, ksel=ksel, t_new=t_new),
        grid=(nb // SQ,),
        in_specs=[pl.BlockSpec((SQ, QR, L), lambda i: (i, 0, 0)),
                  pl.BlockSpec((SQ, IDX_HEADS * QR, IDX_DIM), lambda i: (i, 0, 0)),
                  pl.BlockSpec((SQ, IDX_HEADS * QR, 128), lambda i: (i, 0, 0)),
                  pl.BlockSpec((SQ, IDX_DIM, PAGE_SIZE), lambda i: (i, 0, 0))],
        out_specs=pl.BlockSpec((SQ, QR, Lp), lambda i: (i, 0, 0)),
        out_shape=jax.ShapeDtypeStruct((nb, QR, Lp), F32),
        compiler_params=_cparams(("parallel",)), name="dsa_sample_select",
    )(keys, qi32, w32, ki_new_t)


def _dsa_s_attn_body(pt_ref, q_ref, mb_ref, mbn_ref, kvn_ref, bias_ref, *rest):
    kpages = rest[:PG]
    vpages = rest[PG:2 * PG]
    o_ref = rest[2 * PG]
    m_scr, l_scr, acc_scr = rest[2 * PG + 1:]
    j = pl.program_id(1)
    last = j == pl.num_programs(1) - 1
    scale = (B_HEAD_DIM ** -0.5) * LOG2E

    @pl.when(j == 0)
    def _init():
        m_scr[...] = jnp.full(m_scr.shape, NEG, F32)
        l_scr[...] = jnp.zeros(l_scr.shape, F32)
        acc_scr[...] = jnp.zeros(acc_scr.shape, F32)

    R = B_GROUP * QR
    nrep = B_KV_HEADS * B_GROUP
    qs = [(q_ref[g] * scale).astype(BF16) for g in range(B_KV_HEADS)]

    def near_bias(kind):
        return jnp.concatenate([bias_ref[h, kind] for h in range(B_HEADS)], axis=0)

    def update(s, vs):
        m_old = m_scr[...]
        m_new = jnp.maximum(m_old, jnp.max(s, axis=1, keepdims=True))
        alpha = jnp.exp2(m_old - m_new)
        p = jnp.exp2(s - m_new)
        l_scr[...] = alpha * l_scr[...] + jnp.sum(p, axis=1, keepdims=True)
        pb = p.astype(BF16)
        pv = jnp.concatenate([jnp.dot(pb[g * R:(g + 1) * R], vs[g], preferred_element_type=F32)
                              for g in range(B_KV_HEADS)], axis=0)
        acc_scr[...] = alpha * acc_scr[...] + pv
        m_scr[...] = m_new

    ks = [jnp.concatenate([kpages[p][pl.ds(g, PAGE_SIZE, stride=B_KV_HEADS), :].astype(BF16)
                           for p in range(PG)], axis=0) for g in range(B_KV_HEADS)]
    vs = [jnp.concatenate([vpages[p][pl.ds(g, PAGE_SIZE, stride=B_KV_HEADS), :].astype(BF16)
                           for p in range(PG)], axis=0) for g in range(B_KV_HEADS)]
    s = jnp.concatenate([lax.dot_general(qs[g], ks[g], NT_DIMS, preferred_element_type=F32)
                         for g in range(B_KV_HEADS)], axis=0)
    s = s + jnp.concatenate([mb_ref[...]] * nrep, axis=0)
    nb0 = jnp.where(last, near_bias(0), 0.0)
    s = jnp.concatenate([s[:, :(PG - 1) * PAGE_SIZE], s[:, (PG - 1) * PAGE_SIZE:] + nb0], axis=1)
    update(s, vs)

    @pl.when(last)
    def _fin():
        CW = B_KV_HEADS * B_HEAD_DIM
        kn = [kvn_ref[:, g * B_HEAD_DIM:(g + 1) * B_HEAD_DIM].astype(BF16) for g in range(B_KV_HEADS)]
        vn = [kvn_ref[:, CW + g * B_HEAD_DIM:CW + (g + 1) * B_HEAD_DIM].astype(BF16) for g in range(B_KV_HEADS)]
        sn = jnp.concatenate([lax.dot_general(qs[g], kn[g], NT_DIMS, preferred_element_type=F32)
                              for g in range(B_KV_HEADS)], axis=0)
        sn = sn + jnp.concatenate([mbn_ref[...]] * nrep, axis=0) + near_bias(1)
        update(sn, vn)
        out = acc_scr[...] * (1.0 / l_scr[...])
        for g in range(B_KV_HEADS):
            o_ref[g] = out[g * R:(g + 1) * R]


def _dsa_s_attn(page_table, q32, mb, kv_new, bias_s, ck_rows, cv_rows):
    nb, n_pages = page_table.shape
    nj = n_pages // PG
    R = B_GROUP * QR
    PR = PAGE_SIZE * B_KV_HEADS

    def page_spec(p):
        return pl.BlockSpec((PR, B_HEAD_DIM), lambda b, j, pt: (pt[b, j * PG + p], 0))

    grid_spec = pltpu.PrefetchScalarGridSpec(
        num_scalar_prefetch=1, grid=(nb, nj),
        in_specs=[pl.BlockSpec((None, B_KV_HEADS, R, B_HEAD_DIM), lambda b, j, pt: (b, 0, 0, 0)),
                  pl.BlockSpec((None, QR, PG * PAGE_SIZE), lambda b, j, pt: (b, 0, j)),
                  pl.BlockSpec((None, QR, PAGE_SIZE), lambda b, j, pt: (b, 0, n_pages)),
                  pl.BlockSpec((None, PAGE_SIZE, 2 * B_KV_HEADS * B_HEAD_DIM), lambda b, j, pt: (b, 0, 0)),
                  pl.BlockSpec(bias_s.shape, lambda b, j, pt: (0, 0, 0, 0))]
                 + [page_spec(p) for p in range(PG)] + [page_spec(p) for p in range(PG)],
        out_specs=pl.BlockSpec((None, B_KV_HEADS, R, B_HEAD_DIM), lambda b, j, pt: (b, 0, 0, 0)),
        scratch_shapes=[pltpu.VMEM((B_KV_HEADS * R, 1), F32), pltpu.VMEM((B_KV_HEADS * R, 1), F32),
                        pltpu.VMEM((B_KV_HEADS * R, B_HEAD_DIM), F32)])
    return pl.pallas_call(
        _dsa_s_attn_body, grid_spec=grid_spec,
        out_shape=jax.ShapeDtypeStruct((nb, B_KV_HEADS, R, B_HEAD_DIM), F32),
        compiler_params=_cparams(("parallel", "arbitrary")), name="dsa_sample_attn",
    )(page_table, q32, mb, mb, kv_new, bias_s, *([ck_rows] * PG), *([cv_rows] * PG))


def _layer_norm(r, g, b):
    mu = jnp.mean(r, axis=-1, keepdims=True)
    d = r - mu
    var = jnp.mean(d * d, axis=-1, keepdims=True)
    return d * lax.rsqrt(var + LN_EPS) * g + b


def _merge_body(oa_ref, ob_ref, gates_ref, x_ref, g1_ref, sh2_ref, sc2_ref, w_ref, lng_ref, lnb_ref,
                x1_ref, h2_ref, *, alpha):
    D = oa_ref.shape[1]
    merged = _sigmoid(gates_ref[:, 0:D]) * oa_ref[...] + _sigmoid(gates_ref[:, D:2 * D]) * ob_ref[...]
    y = jnp.dot(merged.astype(BF16), w_ref[...], preferred_element_type=F32)
    x1 = _layer_norm(alpha * x_ref[...] + g1_ref[...] * y, lng_ref[...], lnb_ref[...])
    x1_ref[...] = x1
    h2_ref[...] = (x1 * (1.0 + sc2_ref[...]) + sh2_ref[...]).astype(BF16)


def _merge(oa, ob, z, x2d, mod, rows_per_mod, w_out_bf, ln_g, ln_b, alpha, tm):
    N, D = x2d.shape
    row = lambda i: (i, 0)
    const = lambda i: (0, 0)
    return pl.pallas_call(
        functools.partial(_merge_body, alpha=alpha), grid=(N // tm,),
        in_specs=[pl.BlockSpec((tm, D), row), pl.BlockSpec((tm, D), row),
                  pl.BlockSpec((tm, 2 * D), lambda i: (i, Z_GATES // (2 * D))),
                  pl.BlockSpec((tm, D), row),
                  _mod_spec(mod, 2, tm, rows_per_mod, D, 1),
                  _mod_spec(mod, 3, tm, rows_per_mod, D, 1),
                  _mod_spec(mod, 4, tm, rows_per_mod, D, 1),
                  pl.BlockSpec((D, D), const), pl.BlockSpec((1, D), const), pl.BlockSpec((1, D), const)],
        out_specs=[pl.BlockSpec((tm, D), row), pl.BlockSpec((tm, D), row)],
        out_shape=[jax.ShapeDtypeStruct((N, D), F32), jax.ShapeDtypeStruct((N, D), BF16)],
        compiler_params=_cparams(("parallel",)), name="merge_out",
    )(oa, ob, z, x2d, mod, mod, mod, w_out_bf, ln_g, ln_b)


def _top_values(x, n, with_rank=False):
    out = []
    rank = jnp.full(x.shape, float(n), F32) if with_rank else None
    for r in range(n):
        m = jnp.max(x, axis=0, keepdims=True)
        out.append(m)
        if with_rank:
            rank = jnp.where(x == m, float(r), rank)
        if r < n - 1:
            x = jnp.where(x == m, -jnp.inf, x)
    return (out, rank) if with_rank else out


_PEER_PAIRS = [(a, b) for a in range(PEER_TOPK) for b in range(PEER_TOPK) if (a + 1) * (b + 1) <= PEER_TOPK]


def _peer_body(h2_ref, x1_ref, g2_ref, wq_ref, sk_ref, u0_ref, un_ref, vt_ref, lng_ref, lnb_ref, o_ref,
               np_scr, r2_scr, e1_scr, e2_scr, acc_scr, act_a, act_b, w_ab, *, tn, ec, nchunks, alpha):
    e = pl.program_id(1)
    ne = pl.num_programs(1) - 1
    n_i1 = ec // N_KEYS

    @pl.when(e == 0)
    def _select():
        q = jnp.dot(h2_ref[...], wq_ref[...], preferred_element_type=F32)
        for h in range(PEER_HEADS):
            sc = []
            for c in range(2):
                qhc = q[:, (2 * h + c) * N_KEYS:(2 * h + c + 1) * N_KEYS].astype(BF16)
                sc.append(lax.dot_general(sk_ref[c], qhc, NT_DIMS, preferred_element_type=F32))
            s1, s2 = sc
            t1 = _top_values(s1, PEER_TOPK)
            t2, rank2 = _top_values(s2, PEER_TOPK, with_rank=True)
            cand = jnp.concatenate([t1[a] + t2[b] for a, b in _PEER_PAIRS], axis=0)
            thr = _top_values(cand, PEER_TOPK)[-1]
            mx = t1[0] + t2[0]
            zsum = jnp.sum(jnp.where(cand >= thr, jnp.exp(cand - mx), 0.0), axis=0, keepdims=True)
            npass = jnp.zeros(s1.shape, F32)
            for r in range(PEER_TOPK):
                npass = npass + jnp.where(s1 + t2[r] >= thr, 1.0, 0.0)
            np_scr[h] = npass
            r2_scr[h] = rank2.astype(BF16)
            e1_scr[h] = jnp.exp(s1 - t1[0]) * (1.0 / zsum)
            e2_scr[h] = jnp.exp(s2 - t2[0]).astype(BF16)
        act_a[...] = lax.dot_general(u0_ref[...], h2_ref[...], NT_DIMS, preferred_element_type=F32)
        w_ab[1] = jnp.zeros(w_ab.shape[1:], BF16)
        acc_scr[...] = jnp.zeros(acc_scr.shape, F32)

    PK = 16

    def chunk(act_cur, act_nxt, w_cur, w_prev):
        acc_scr[...] += jnp.dot(vt_ref[...], w_prev[...], preferred_element_type=F32)

        act_nxt[...] = lax.dot_general(un_ref[...], h2_ref[...], NT_DIMS, preferred_element_type=F32)
        for jj in range(n_i1):
            i1 = e * n_i1 + jj
            a = act_cur[jj * N_KEYS:(jj + 1) * N_KEYS, :]
            gsum = jnp.zeros((N_KEYS // PK, PK, tn), BF16)
            for h in range(PEER_HEADS):
                npass_row = jnp.broadcast_to(np_scr[h, pl.ds(i1, 1), :], (PK, tn)).astype(BF16)
                e1_row = jnp.broadcast_to(e1_scr[h, pl.ds(i1, 1), :], (PK, tn)).astype(BF16)
                sel = r2_scr[h].reshape(N_KEYS // PK, PK, tn) < npass_row
                e2 = e2_scr[h].reshape(N_KEYS // PK, PK, tn)
                gsum = gsum + jnp.where(sel, e2, jnp.zeros_like(e2)) * e1_row
            gelu = 0.5 * a * (1.0 + lax.erf(a * (2.0 ** -0.5)))
            w_cur[jj * N_KEYS:(jj + 1) * N_KEYS, :] = gsum.reshape(N_KEYS, tn) * gelu.astype(BF16)

    @pl.when((e < ne) & (e % 2 == 0))
    def _even():
        chunk(act_a, act_b, w_ab.at[e % 2], w_ab.at[1 - e % 2])

    @pl.when((e < ne) & (e % 2 == 1))
    def _odd():
        chunk(act_b, act_a, w_ab.at[e % 2], w_ab.at[1 - e % 2])

    @pl.when(e == ne)
    def _fin():
        acc = acc_scr[...] + jnp.dot(vt_ref[...], w_ab[(nchunks - 1) % 2], preferred_element_type=F32)
        y = acc.T
        o_ref[...] = _layer_norm(alpha * x1_ref[...] + g2_ref[...] * y, lng_ref[...], lnb_ref[...])


def _peer(h2, x1, mod, rows_per_mod, wq_bf, sk_bf, u_bf, vt_bf, ln_g, ln_b, alpha, tn, ec):
    N, D = x1.shape
    E = u_bf.shape[0]
    ne = E // ec
    row = lambda i, e: (i, 0)
    const2 = lambda i, e: (0, 0)
    if mod.ndim == 3:
        g2_spec = pl.BlockSpec((None, 1, D), lambda i, e: ((i * tn) // rows_per_mod, 0, 5))
    else:
        g2_spec = pl.BlockSpec((tn, D), lambda i, e: (i, 5))
    return pl.pallas_call(
        functools.partial(_peer_body, tn=tn, ec=ec, nchunks=ne, alpha=alpha), grid=(N // tn, ne + 1),
        in_specs=[pl.BlockSpec((tn, D), row), pl.BlockSpec((tn, D), row), g2_spec,
                  _resident(wq_bf.shape, const2),
                  _resident(sk_bf.shape, lambda i, e: (0, 0, 0)),
                  _resident((ec, D), const2),
                  pl.BlockSpec((ec, D), lambda i, e: (jnp.minimum(e + 1, ne - 1), 0)),
                  pl.BlockSpec((D, ec), lambda i, e: (0, jnp.maximum(e - 1, 0))),
                  pl.BlockSpec((1, D), const2), pl.BlockSpec((1, D), const2)],
        out_specs=pl.BlockSpec((tn, D), row),
        out_shape=jax.ShapeDtypeStruct((N, D), F32),
        scratch_shapes=[pltpu.VMEM((PEER_HEADS, N_KEYS, tn), F32), pltpu.VMEM((PEER_HEADS, N_KEYS, tn), BF16),
                        pltpu.VMEM((PEER_HEADS, N_KEYS, tn), F32), pltpu.VMEM((PEER_HEADS, N_KEYS, tn), BF16),
                        pltpu.VMEM((D, tn), F32),
                        pltpu.VMEM((ec, tn), F32), pltpu.VMEM((ec, tn), F32), pltpu.VMEM((2, ec, tn), BF16)],
        compiler_params=_cparams(("parallel", "arbitrary")), name="peer",
    )(h2, x1, mod, wq_bf, sk_bf, u_bf, u_bf, vt_bf, ln_g, ln_b)


def _pack_w_in(w_in):
    D = w_in.shape[0]
    idx_w = w_in[:, 5632:5956]
    pad = jnp.zeros((D, Z_GATES - Z_QI - idx_w.shape[1]), w_in.dtype)
    return jnp.concatenate([w_in[:, :5632], idx_w, pad, w_in[:, 5956:]], axis=1).astype(BF16)


def _prompt_buckets(tq):
    i = np.arange(tq)[:, None]
    j = np.arange(tq)[None, :]
    return np.stack([_t5_bucket_np(np.maximum(i - j, 0)), _t5_bucket_np(tq + i - j)])


def _sample_buckets():
    t = np.arange(QR)[:, None]
    o = np.arange(PAGE_SIZE)[None, :]
    return np.stack([_t5_bucket_np(PAGE_SIZE + t - o), _t5_bucket_np(np.maximum(t - o, 0))])


def _layer_tail(oa, ob, z, x2d, mod, rows_per_mod, wts, alpha, tm, tn_peer):
    x1, h2 = _merge(oa, ob, z, x2d, mod, rows_per_mod, wts["w_out"], wts["ln1_g"], wts["ln1_b"], alpha, tm)
    return _peer(h2, x1, mod, rows_per_mod, wts["w_peer_q"], wts["sub_keys"], wts["peer_u"], wts["peer_vt"],
                 wts["ln2_g"], wts["ln2_b"], alpha, tn_peer, 1024)


def kernel(x_prompt, x_sample, cache_k, cache_v, cache_kidx, state_hgrn, page_table, c_prompt, c_sample,
           w_ada, b_ada, w_in, lb_param, a_gnorm, t5_table, w_out, ln1_g, ln1_b, w_peer_q, peer_sub_keys,
           peer_u, peer_v, ln2_g, ln2_b):
    depth = w_ada.shape[0]
    assert depth == 1 and lb_param.shape[0] == 2
    alpha = (2.0 * depth) ** 0.25
    B, T, D = x_prompt.shape
    NB, TS, _ = x_sample.shape
    n_pages = page_table.shape[1]
    past_len = n_pages * PAGE_SIZE

    wts = dict(
        w_out=w_out[0].astype(BF16), ln1_g=ln1_g, ln1_b=ln1_b, ln2_g=ln2_g, ln2_b=ln2_b,
        w_peer_q=w_peer_q[0].astype(BF16), sub_keys=peer_sub_keys[0].astype(BF16),
        peer_u=peer_u[0].astype(BF16), peer_vt=peer_v[0].T.astype(BF16))
    w_in_bf = _pack_w_in(w_in[0])
    gnorm = a_gnorm.reshape(1, A_DK)

    mod = _ada(jnp.concatenate([c_prompt, c_sample], axis=0), w_ada[0].astype(BF16), b_ada)
    mod_p = mod[:B].reshape(B, 1, 6 * D)
    mod_s = jnp.repeat(mod[B:], TS, axis=0)

    xp = x_prompt.reshape(B * T, D)
    zp, kp_rows, vp_rows = _inproj(xp, mod_p, T, w_in_bf, 256, 1024)
    oa_p, s_p = _hgrn(zp, jnp.zeros((B, A_HEADS, A_DK, A_DK), F32), lb_param, gnorm, B, T, 256, A_CHUNK, 256, 4)
    tq = 256
    bias_p = _bias_tiles(t5_table, _prompt_buckets(tq))
    bias_g = bias_p.reshape(B_KV_HEADS, B_GROUP, 2, tq, tq).transpose(0, 2, 1, 3, 4)
    bias_g = bias_g.reshape(B_KV_HEADS, 2, B_GROUP * tq, tq)
    bias_g = jnp.concatenate([bias_g, jnp.zeros((B_KV_HEADS, 1, B_GROUP * tq, tq), F32)], axis=1)
    ob_p = _dsa_prompt(zp, bias_g, B, T, tq)
    y_p = _layer_tail(oa_p, ob_p, zp, xp, mod_p, T, wts, alpha, 256, 512)

    xs = x_sample.reshape(NB * TS, D)
    zs, ks_rows, vs_rows = _inproj(xs, mod_s, 1, w_in_bf, min(256, NB * TS), 1024)
    zs3 = zs.reshape(NB, TS, Z_WIDTH)
    zs_pad = jnp.pad(zs3, ((0, 0), (0, QR - TS), (0, 0))).reshape(NB * QR, Z_WIDTH)
    assert TS <= QR
    oa_s8, s_s = _hgrn(zs_pad, state_hgrn[0], lb_param, gnorm, NB, QR, QR, QR, TS, A_HEADS)
    oa_s = oa_s8.reshape(NB, QR, D)[:, :TS].reshape(NB * TS, D)

    qi_s = zs3[:, :, Z_QI:Z_QI + IDX_HEADS * IDX_DIM].reshape(NB, TS, IDX_HEADS, IDX_DIM)
    qi32 = jnp.pad(qi_s.transpose(0, 2, 1, 3), ((0, 0), (0, 0), (0, QR - TS), (0, 0)))
    qi32 = qi32.reshape(NB, IDX_HEADS * QR, IDX_DIM).astype(BF16)
    wi_s = zs3[:, :, Z_WI:Z_WI + IDX_HEADS].transpose(0, 2, 1)
    w32 = jnp.pad(wi_s, ((0, 0), (0, 0), (0, QR - TS))).reshape(NB, IDX_HEADS * QR, 1)
    w32 = jnp.broadcast_to(w32, (NB, IDX_HEADS * QR, 128))
    ki_new = jnp.pad(zs3[:, :, Z_KI:Z_KI + IDX_DIM], ((0, 0), (0, PAGE_SIZE - TS), (0, 0))).astype(BF16)
    ki_new_t = jnp.swapaxes(ki_new, 1, 2)
    kv_new = jnp.pad(zs3[:, :, Z_KB:Z_KB + 512], ((0, 0), (0, PAGE_SIZE - TS), (0, 0)))
    q_s = zs3[:, :, Z_QB:Z_QB + 1024].reshape(NB, TS, B_KV_HEADS, B_GROUP, B_HEAD_DIM)
    q32 = jnp.pad(q_s.transpose(0, 2, 3, 1, 4), ((0, 0), (0, 0), (0, 0), (0, QR - TS), (0, 0)))
    q32 = q32.reshape(NB, B_KV_HEADS, B_GROUP * QR, B_HEAD_DIM)

    ksel = min(TOPK_MAX, (past_len + TS) // 4)
    kidx_t = jnp.swapaxes(cache_kidx[0], 1, 2)
    keys_s = _dsa_s_scores(page_table, qi32, w32, kidx_t)
    mb_s = _dsa_s_select(keys_s, qi32, w32, ki_new_t, ksel, TS)
    bias_s = _bias_tiles(t5_table, _sample_buckets())
    n_pool = cache_k.shape[1]
    ck = cache_k.reshape(n_pool * PAGE_SIZE * B_KV_HEADS, B_HEAD_DIM)
    cv = cache_v.reshape(n_pool * PAGE_SIZE * B_KV_HEADS, B_HEAD_DIM)
    o32 = _dsa_s_attn(page_table, q32, mb_s, kv_new, bias_s, ck, cv)
    ob_s = o32.reshape(NB, B_KV_HEADS, B_GROUP, QR, B_HEAD_DIM)[:, :, :, :TS]
    ob_s = ob_s.transpose(0, 3, 1, 2, 4).reshape(NB * TS, D)
    y_s = _layer_tail(oa_s, ob_s, zs, xs, mod_s, 1, wts, alpha, NB * TS, NB * TS)

    def kv_out(z, k_rows, v_rows, nb, t):
        k = k_rows.reshape(1, nb, t, B_KV_HEADS, B_HEAD_DIM)
        v = v_rows.reshape(1, nb, t, B_KV_HEADS, B_HEAD_DIM)
        ki = z[:, Z_KI:Z_KI + IDX_DIM].reshape(1, nb, t, IDX_DIM)
        return k, v, ki

    kp, vp, ip = kv_out(zp, kp_rows, vp_rows, B, T)
    ks, vs, is_ = kv_out(zs, ks_rows, vs_rows, NB, TS)
    return (y_p.reshape(B, T, D), y_s.reshape(NB, TS, D), kp, vp, ip, s_p[None],
            ks, vs, is_, s_s[None])
```

```python
import functools
import math

import numpy as np
import jax
import jax.numpy as jnp
from jax import lax
from jax.experimental import pallas as pl
from jax.experimental.pallas import tpu as pltpu

F32 = jnp.float32
BF16 = jnp.bfloat16
I32 = jnp.int32

A_HEADS = 8
A_DK = 128
A_CHUNK = 32
B_HEADS = 8
B_KV_HEADS = 2
B_GROUP = 4
B_HEAD_DIM = 128
IDX_HEADS = 4
IDX_DIM = 64
TOPK_MAX = 256
N_BUCKETS = 32
MAX_DISTANCE = 128
PEER_HEADS = 8
N_KEYS = 128
PEER_TOPK = 16
PAGE_SIZE = 128
LN_EPS = 1e-5
RMS_EPS = 1e-6

Z_A = 0
Z_QB = 4096
Z_KB = 5120
Z_VB = 5376
Z_QI = 5632
Z_KI = 5888
Z_WI = 5952
Z_GATES = 6144
Z_WIDTH = 8192

NEG = -1e30
INT_MIN = -2 ** 31
LOG2E = 1.4426950408889634
NT_DIMS = (((1,), (1,)), ((), ()))
VMEM_LIMIT = 56 * 1024 * 1024


def _cparams(sem):
    return pltpu.CompilerParams(dimension_semantics=sem, vmem_limit_bytes=VMEM_LIMIT)


def _resident(shape, index_map):
    return pl.BlockSpec(shape, index_map, pipeline_mode=pl.Buffered(1))


def _sigmoid(x):
    return jax.nn.sigmoid(x)


def _ada_body(c_ref, w_ref, b_ref, o_ref):
    c = c_ref[...]
    s = c * _sigmoid(c)
    o_ref[...] = jnp.dot(s.astype(BF16), w_ref[...], preferred_element_type=F32) + b_ref[...]


def _ada(c, w_bf, b):
    R, D = c.shape
    N = w_bf.shape[1]
    tn = N // 4
    return pl.pallas_call(
        _ada_body, grid=(N // tn,),
        in_specs=[pl.BlockSpec((R, D), lambda j: (0, 0)),
                  pl.BlockSpec((D, tn), lambda j: (0, j)),
                  pl.BlockSpec((1, tn), lambda j: (0, j))],
        out_specs=pl.BlockSpec((R, tn), lambda j: (0, j)),
        out_shape=jax.ShapeDtypeStruct((R, N), F32),
        compiler_params=_cparams(("parallel",)), name="ada")(c, w_bf, b)


def _mod_spec(mod, blk, tm, rows_per_mod, D, nidx):
    if mod.ndim == 3:
        if nidx == 1:
            return pl.BlockSpec((None, 1, D), lambda i: ((i * tm) // rows_per_mod, 0, blk))
        return pl.BlockSpec((None, 1, D), lambda i, j: ((i * tm) // rows_per_mod, 0, blk))
    if nidx == 1:
        return pl.BlockSpec((tm, D), lambda i: (i, blk))
    return pl.BlockSpec((tm, D), lambda i, j: (i, blk))


def _inproj_body(x_ref, sh_ref, sc_ref, w_ref, o_ref, k_ref, v_ref, *, tn):
    tm = x_ref.shape[0]
    h = (x_ref[...] * (1.0 + sc_ref[...]) + sh_ref[...]).astype(BF16)
    for j in range(o_ref.shape[1] // tn):
        zt = jnp.dot(h, w_ref[:, j * tn:(j + 1) * tn], preferred_element_type=F32)
        o_ref[:, j * tn:(j + 1) * tn] = zt
        if j * tn <= Z_KB and Z_VB + B_KV_HEADS * B_HEAD_DIM <= (j + 1) * tn:
            for g in range(B_KV_HEADS):
                ko = Z_KB - j * tn + g * B_HEAD_DIM
                vo = Z_VB - j * tn + g * B_HEAD_DIM
                k_ref[pl.ds(g, tm, stride=B_KV_HEADS), :] = zt[:, ko:ko + B_HEAD_DIM]
                v_ref[pl.ds(g, tm, stride=B_KV_HEADS), :] = zt[:, vo:vo + B_HEAD_DIM]


def _inproj(x2d, mod, rows_per_mod, w_bf, tm, tn):
    N, D = x2d.shape
    W = w_bf.shape[1]
    kv_spec = pl.BlockSpec((tm * B_KV_HEADS, B_HEAD_DIM), lambda i: (i, 0))
    kv_shape = jax.ShapeDtypeStruct((N * B_KV_HEADS, B_HEAD_DIM), F32)
    return pl.pallas_call(
        functools.partial(_inproj_body, tn=tn), grid=(N // tm,),
        in_specs=[pl.BlockSpec((tm, D), lambda i: (i, 0)),
                  _mod_spec(mod, 0, tm, rows_per_mod, D, 1),
                  _mod_spec(mod, 1, tm, rows_per_mod, D, 1),
                  _resident((D, W), lambda i: (0, 0))],
        out_specs=[pl.BlockSpec((tm, W), lambda i: (i, 0)), kv_spec, kv_spec],
        out_shape=[jax.ShapeDtypeStruct((N, W), F32), kv_shape, kv_shape],
        compiler_params=_cparams(("parallel",)), name="inproj")(x2d, mod, mod, w_bf)


def _hgrn_body(q_ref, f_ref, i_ref, g_ref, lbp_ref, gn_ref, s0_ref, o_ref, sout_ref, st_scr,
               *, C, Tb, t_valid, HB):
    tb = pl.program_id(2)
    shift = int(math.log2(C))
    nchunk = Tb // C
    r = lax.broadcasted_iota(I32, (Tb, Tb), 0)
    c = lax.broadcasted_iota(I32, (Tb, Tb), 1)
    tril = (jnp.right_shift(r, shift) == jnp.right_shift(c, shift)) & (c <= r)
    trilf = jnp.where(tril, 1.0, 0.0)
    Tp = max(Tb, 128)
    colchunk = jnp.right_shift(lax.broadcasted_iota(I32, (A_DK, Tp), 1), shift)

    lbp = lbp_ref[...]
    mx = jnp.max(lbp, axis=0, keepdims=True)
    e = jnp.exp(lbp - mx)
    lb_all = e[0:1, :] / jnp.sum(e, axis=0, keepdims=True)

    @pl.when(tb == 0)
    def _init():
        for hh in range(HB):
            st_scr[hh] = s0_ref[hh].T

    W = HB * A_DK
    f = lb_all + (1.0 - lb_all) * _sigmoid(f_ref[...])
    lf = jnp.log(f)
    k = 1.0 - f
    if t_valid < Tb:
        ok = lax.broadcasted_iota(I32, (Tb, W), 0) < t_valid
        lf = jnp.where(ok, lf, 0.0)
        k = jnp.where(ok, k, 0.0)
    bcum = jnp.dot(trilf, lf, precision=lax.Precision.HIGHEST,
                   preferred_element_type=F32)
    tot = jnp.concatenate(
        [jnp.broadcast_to(bcum[(j + 1) * C - 1:(j + 1) * C, :], (C, W)) for j in range(nchunk)], axis=0)
    qa = q_ref[...]
    qe_b = ((qa * _sigmoid(qa)) * jnp.exp(bcum)).astype(BF16)
    ke_b = (k * jnp.exp(-bcum)).astype(BF16)
    kd = k * jnp.exp(tot - bcum)
    v = i_ref[...]
    v_b = v.astype(BF16)
    if Tb < 128:
        zpad = jnp.zeros((128 - Tb, W), F32)
        v_p = jnp.concatenate([v, zpad], axis=0)
        kd_p = jnp.concatenate([kd, zpad], axis=0)
    else:
        v_p, kd_p = v, kd
    kd_b = kd_p.astype(BF16)
    heads = [(hh * A_DK, (hh + 1) * A_DK) for hh in range(HB)]

    scores = [jnp.where(tril, lax.dot_general(qe_b[:, lo:hi], ke_b[:, lo:hi], NT_DIMS,
                                              preferred_element_type=F32), 0.0).astype(BF16) for lo, hi in heads]
    o = [jnp.dot(scores[hh], v_b[:, lo:hi], preferred_element_type=F32) for hh, (lo, hi) in enumerate(heads)]
    vT = [v_p[:, lo:hi].T for lo, hi in heads]
    ST = [st_scr[hh] for hh in range(HB)]
    o_inter = [[] for _ in range(HB)]
    for j in range(nchunk):
        for hh, (lo, hi) in enumerate(heads):
            o_inter[hh].append(lax.dot_general(qe_b[j * C:(j + 1) * C, lo:hi], ST[hh].astype(BF16), NT_DIMS,
                                               preferred_element_type=F32))
            d_c = jnp.exp(tot[j * C:j * C + 1, lo:hi])
            vTm = jnp.where(colchunk == j, vT[hh], 0.0) if nchunk > 1 else vT[hh]
            ST[hh] = ST[hh] * d_c + jnp.dot(vTm.astype(BF16), kd_b[:, lo:hi], preferred_element_type=F32)
    gn = gn_ref[...]
    ga = g_ref[...]
    gate = ga * _sigmoid(ga)
    for hh, (lo, hi) in enumerate(heads):
        st_scr[hh] = ST[hh]
        oh = o[hh] + jnp.concatenate(o_inter[hh], axis=0)
        ms = jnp.mean(oh * oh, axis=-1, keepdims=True)
        o_ref[:, lo:hi] = oh * lax.rsqrt(ms + RMS_EPS) * gn * gate[:, lo:hi]

    @pl.when(tb == pl.num_programs(2) - 1)
    def _fin():
        for hh in range(HB):
            sout_ref[hh] = st_scr[hh].T


def _hgrn(z, s0, lb_param, gnorm, B, T, Tb, C, t_valid, HB):
    nt = T // Tb
    H = A_HEADS
    W = HB * A_DK
    ng = H // HB

    def col(group):
        return pl.BlockSpec((Tb, W), lambda b, h, t: (b * nt + t, group * ng + h))

    return pl.pallas_call(
        functools.partial(_hgrn_body, C=C, Tb=Tb, t_valid=t_valid, HB=HB),
        grid=(B, ng, nt),
        in_specs=[col(0), col(1), col(2), col(3),
                  pl.BlockSpec((2, W), lambda b, h, t: (0, h)),
                  pl.BlockSpec((1, A_DK), lambda b, h, t: (0, 0)),
                  pl.BlockSpec((None, HB, A_DK, A_DK), lambda b, h, t: (b, h, 0, 0))],
        out_specs=[pl.BlockSpec((Tb, W), lambda b, h, t: (b * nt + t, h)),
                   pl.BlockSpec((None, HB, A_DK, A_DK), lambda b, h, t: (b, h, 0, 0))],
        out_shape=[jax.ShapeDtypeStruct((B * T, H * A_DK), F32),
                   jax.ShapeDtypeStruct((B, H, A_DK, A_DK), F32)],
        scratch_shapes=[pltpu.VMEM((HB, A_DK, A_DK), F32)],
        compiler_params=_cparams(("parallel", "parallel", "arbitrary")), name="hgrn",
    )(z, z, z, z, lb_param, gnorm, s0)


def _t5_bucket_np(rel):
    rel = np.asarray(rel, np.int64)
    max_exact = N_BUCKETS // 2
    relf = np.maximum(rel, 1).astype(np.float64)
    large = max_exact + (np.log(relf / max_exact) / math.log(MAX_DISTANCE / max_exact)
                         * (N_BUCKETS - max_exact)).astype(np.int64)
    large = np.minimum(large, N_BUCKETS - 1)
    return np.where(rel < max_exact, rel, large).astype(np.int32)


def _bias_body(tab_ref, bkt_ref, o_ref):
    h = pl.program_id(0)
    b = bkt_ref[...]
    acc = jnp.zeros(b.shape, F32)
    for i in range(N_BUCKETS):
        acc = jnp.where(b == i, tab_ref[i, h], acc)
    o_ref[...] = (acc - tab_ref[N_BUCKETS - 1, h]) * LOG2E


def _bias_tiles(t5_table, buckets):
    K, R, Cc = buckets.shape
    return pl.pallas_call(
        _bias_body, grid=(B_HEADS, K),
        in_specs=[pl.BlockSpec(memory_space=pltpu.SMEM),
                  pl.BlockSpec((None, R, Cc), lambda h, k: (k, 0, 0))],
        out_specs=pl.BlockSpec((None, None, R, Cc), lambda h, k: (h, k, 0, 0)),
        out_shape=jax.ShapeDtypeStruct((B_HEADS, K, R, Cc), F32),
        compiler_params=_cparams(("parallel", "parallel")), name="t5_bias")(t5_table, jnp.asarray(buckets))


def _ordered_key(score):
    score = jnp.where(score == 0.0, 0.0, score)
    bits = pltpu.bitcast(score, I32)
    return jnp.where(bits < 0, jnp.bitwise_xor(bits, jnp.int32(0x7FFFFFFF)), bits)


def _bcast_lanes(col, n):
    return jnp.broadcast_to(col, (col.shape[0], n))


def _dsa_prompt_body(q_ref, kv_ref, idxq_ref, ki_ref, bias_ref, o_ref,
                     key_scr, mb_scr, m_scr, l_scr, acc_scr, s_scr, *, tq, ksel):
    qi = pl.program_id(1)
    nvalid = qi + 1
    npair = (nvalid + 1) // 2
    tw = 2 * tq

    idxq = idxq_ref[...]
    assert (IDX_DIM ** -0.5) == 0.125 and (IDX_HEADS ** -0.5) == 0.5
    q_idx = [(idxq[:, h * IDX_DIM:(h + 1) * IDX_DIM] * (IDX_DIM ** -0.5)).astype(BF16) for h in range(IDX_HEADS)]
    w_t = idxq[:, Z_KI - Z_QI:Z_KI - Z_QI + 128].T
    w_rows = [w_t[Z_WI - Z_KI + h:Z_WI - Z_KI + h + 1, :] * (IDX_HEADS ** -0.5) for h in range(IDX_HEADS)]
    key_i = lax.broadcasted_iota(I32, (tq, tq), 0)
    qry_t = qi * tq + lax.broadcasted_iota(I32, (tq, tq), 1)

    def idx_chunk(c, carry):
        off = pl.multiple_of(c * tq, tq)
        kic = ki_ref[pl.ds(off, tq), 0:IDX_DIM].astype(BF16)
        acc = jnp.zeros((tq, tq), F32)
        for h in range(IDX_HEADS):
            s = lax.dot_general(kic, q_idx[h], NT_DIMS, preferred_element_type=F32)
            acc = acc + jnp.maximum(s, 0.0) * w_rows[h]
        score = jnp.where(off + key_i <= qry_t, acc, -jnp.inf)
        key_scr[pl.ds(off, tq), :] = _ordered_key(score)
        return carry

    lax.fori_loop(0, nvalid, idx_chunk, 0)

    @pl.when(nvalid % 2 == 1)
    def _pad():
        off = pl.multiple_of(nvalid * tq, tq)
        key_scr[pl.ds(off, tq), :] = jnp.full((tq, tq), INT_MIN, I32)
        mb_scr[:, pl.ds(off, tq)] = jnp.full((tq, tq), NEG, F32)

    NACC = 8

    def count(pred_fn):
        def body(c, cnt):
            off = pl.multiple_of(c * tw, tw)
            x = key_scr[pl.ds(off, tw), :].reshape(tw // (8 * NACC), NACC, 8, tq)
            return cnt + jnp.sum(jnp.where(pred_fn(x), 1.0, 0.0), axis=0)
        cnt = lax.fori_loop(0, npair, body, jnp.zeros((NACC, 8, tq), F32))
        return jnp.sum(jnp.sum(cnt, axis=0), axis=0, keepdims=True)

    def bit_step(it, thr):
        cand = thr + jnp.left_shift(jnp.int32(1), 31 - it)
        return jnp.where(count(lambda x: x >= cand) >= ksel, cand, thr)

    thr = lax.fori_loop(0, 32, bit_step, jnp.full((1, tq), INT_MIN, I32))
    need = ksel - count(lambda x: x > thr)
    tril = jnp.where(lax.broadcasted_iota(I32, (tq, tq), 1) < key_i, 1.0, 0.0).astype(BF16)

    def mask_chunk(c, carry):
        off = pl.multiple_of(c * tq, tq)
        x = key_scr[pl.ds(off, tq), :]
        eq = x == thr
        eqf = jnp.where(eq, 1.0, 0.0)
        rank = jnp.dot(tril, eqf.astype(BF16), preferred_element_type=F32) + carry
        mb = jnp.where(x > thr, 0.0, jnp.where(eq, jnp.where(rank < need, 0.0, NEG), NEG))
        mb = jnp.where(off + key_i <= qry_t, mb, NEG)
        mb_scr[:, pl.ds(off, tq)] = mb.T
        return carry + jnp.sum(eqf, axis=0, keepdims=True)

    lax.fori_loop(0, nvalid, mask_chunk, jnp.zeros((1, tq), F32))

    scale = (B_HEAD_DIM ** -0.5) * LOG2E
    nsl = tw // 128

    def bias_kind(c):
        d = qi - c
        return jnp.where(d == 0, 0, jnp.where(d == 1, 1, 2))

    for g in range(B_KV_HEADS):
        qg = jnp.concatenate(
            [q_ref[:, (g * B_GROUP + r) * B_HEAD_DIM:(g * B_GROUP + r + 1) * B_HEAD_DIM] for r in range(B_GROUP)],
            axis=0)
        qg = (qg * scale).astype(BF16)

        def logits(c2, g=g, qg=qg):
            off = pl.multiple_of(c2 * tw, tw)
            kc = kv_ref[pl.ds(off, tw), g * B_HEAD_DIM:(g + 1) * B_HEAD_DIM].astype(BF16)
            s = lax.dot_general(qg, kc, NT_DIMS, preferred_element_type=F32)
            mb = mb_scr[:, pl.ds(off, tw)]
            bias = jnp.concatenate([bias_ref[g, bias_kind(2 * c2)], bias_ref[g, bias_kind(2 * c2 + 1)]], axis=1)
            return jnp.concatenate([s[r * tq:(r + 1) * tq] + mb for r in range(B_GROUP)], axis=0) + bias

        def max_body(c2, m):
            s = logits(c2)
            s_scr[:, pl.ds(pl.multiple_of(c2 * tw, tw), tw)] = s
            for k in range(nsl):
                m = jnp.maximum(m, s[:, k * 128:(k + 1) * 128])
            return m

        m_scr[...] = lax.fori_loop(0, npair, max_body, jnp.full(m_scr.shape, NEG, F32))
        m_row = _bcast_lanes(jnp.max(m_scr[...], axis=1, keepdims=True), 128)
        m_scr[...] = m_row
        l_scr[...] = jnp.zeros(l_scr.shape, F32)
        acc_scr[...] = jnp.zeros(acc_scr.shape, F32)

        def sum_body(c2, carry, g=g):
            off = pl.multiple_of(c2 * tw, tw)
            vc = kv_ref[pl.ds(off, tw), (B_KV_HEADS + g) * B_HEAD_DIM:(B_KV_HEADS + g + 1) * B_HEAD_DIM].astype(BF16)
            m = m_scr[...]
            p = jnp.exp2(s_scr[:, pl.ds(off, tw)] - jnp.concatenate([m] * nsl, axis=1))
            l = l_scr[...]
            for k in range(nsl):
                l = l + p[:, k * 128:(k + 1) * 128]
            l_scr[...] = l
            acc_scr[...] += jnp.dot(p.astype(BF16), vc, preferred_element_type=F32)
            return carry

        lax.fori_loop(0, npair, sum_body, 0)
        out = acc_scr[...] * (1.0 / jnp.sum(l_scr[...], axis=1, keepdims=True))
        for r in range(B_GROUP):
            h = g * B_GROUP + r
            o_ref[:, h * B_HEAD_DIM:(h + 1) * B_HEAD_DIM] = out[r * tq:(r + 1) * tq]


def _dsa_prompt(z, bias_g, B, T, tq):
    nq = T // tq
    ksel = min(TOPK_MAX, T // 4)
    assert tq >= ksel and tq % 128 == 0 and nq % 2 == 0
    R = B_GROUP * tq
    return pl.pallas_call(
        functools.partial(_dsa_prompt_body, tq=tq, ksel=ksel),
        grid=(B, nq),
        in_specs=[pl.BlockSpec((tq, 1024), lambda b, i: (b * nq + i, Z_QB // 1024)),
                  _resident((T, 512), lambda b, i: (b, Z_KB // 512)),
                  pl.BlockSpec((tq, 512), lambda b, i: (b * nq + i, Z_QI // 512)),
                  _resident((T, 128), lambda b, i: (b, Z_KI // 128)),
                  _resident(bias_g.shape, lambda b, i: (0, 0, 0, 0))],
        out_specs=pl.BlockSpec((tq, 1024), lambda b, i: (b * nq + i, 0)),
        out_shape=jax.ShapeDtypeStruct((B * T, 1024), F32),
        scratch_shapes=[pltpu.VMEM((T, tq), I32), pltpu.VMEM((tq, T), F32),
                        pltpu.VMEM((R, 128), F32), pltpu.VMEM((R, 128), F32), pltpu.VMEM((R, B_HEAD_DIM), F32),
                        pltpu.VMEM((R, T), F32)],
        compiler_params=_cparams(("parallel", "arbitrary")), name="dsa_prompt",
    )(z, z, z, z, bias_g)


PG = 16
QR = 8


def _idx_score(qi32, w32, keys_t_bf):
    s = jnp.dot(qi32, keys_t_bf, preferred_element_type=F32)
    s = jnp.maximum(s * (IDX_DIM ** -0.5), 0.0) * w32
    acc = s[0:QR]
    for h in range(1, IDX_HEADS):
        acc = acc + s[h * QR:(h + 1) * QR]
    return acc * (IDX_HEADS ** -0.5)


def _dsa_s_scores_body(pt_ref, qi_ref, w_ref, *rest, pgs):
    pages, key_ref = rest[:pgs], rest[pgs]
    qi32, w32 = qi_ref[...], w_ref[...]
    for p in range(pgs):
        key_ref[:, p * PAGE_SIZE:(p + 1) * PAGE_SIZE] = _ordered_key(
            _idx_score(qi32, w32, pages[p][...].astype(BF16)))


def _dsa_s_scores(page_table, qi32, w32, kidx_t):
    nb, n_pages = page_table.shape
    pgs = math.gcd(n_pages, 16)
    nj = n_pages // pgs

    def page_spec(p):
        return pl.BlockSpec((None, IDX_DIM, PAGE_SIZE), lambda b, j, pt: (pt[b, j * pgs + p], 0, 0))

    grid_spec = pltpu.PrefetchScalarGridSpec(
        num_scalar_prefetch=1, grid=(nb, nj),
        in_specs=[pl.BlockSpec((None, IDX_HEADS * QR, IDX_DIM), lambda b, j, pt: (b, 0, 0)),
                  pl.BlockSpec((None, IDX_HEADS * QR, 128), lambda b, j, pt: (b, 0, 0))]
                 + [page_spec(p) for p in range(pgs)],
        out_specs=pl.BlockSpec((None, QR, pgs * PAGE_SIZE), lambda b, j, pt: (b, 0, j)))
    return pl.pallas_call(
        functools.partial(_dsa_s_scores_body, pgs=pgs), grid_spec=grid_spec,
        out_shape=jax.ShapeDtypeStruct((nb, QR, n_pages * PAGE_SIZE), I32),
        compiler_params=_cparams(("parallel", "parallel")), name="dsa_sample_scores",
    )(page_table, qi32, w32, *([kidx_t] * pgs))


SQ = 8


def _dsa_s_select_body(key_ref, qi_ref, w_ref, kin_ref, mb_ref, *, n_pages, ksel, t_new):
    R = SQ * QR
    row = lax.broadcasted_iota(I32, (QR, PAGE_SIZE), 0)
    col = lax.broadcasted_iota(I32, (QR, PAGE_SIZE), 1)
    new_ok1 = (col <= row) & (col < t_new)
    key_new = jnp.concatenate(
        [_ordered_key(jnp.where(new_ok1, _idx_score(qi_ref[q], w_ref[q], kin_ref[q]), -jnp.inf)) for q in range(SQ)],
        axis=0)
    new_ok = jnp.concatenate([new_ok1] * SQ, axis=0)

    def slab(s):
        if s == n_pages:
            return key_new
        return key_ref[:, :, s * PAGE_SIZE:(s + 1) * PAGE_SIZE].reshape(R, PAGE_SIZE)

    def count(pred_fn):
        cnt = jnp.zeros((R, 128), F32)
        for s in range(n_pages + 1):
            cnt = cnt + jnp.where(pred_fn(slab(s)), 1.0, 0.0)
        return jnp.sum(cnt, axis=1, keepdims=True)

    def bit_step(it, thr):
        cand = thr + jnp.left_shift(jnp.int32(1), 31 - it)
        tot = count(lambda x: x >= cand)
        return jnp.where(_bcast_lanes(tot, 128) >= ksel, cand, thr)

    thr = lax.fori_loop(0, 32, bit_step, jnp.full((R, 128), INT_MIN, I32))
    need = _bcast_lanes(ksel - count(lambda x: x > thr), 128)
    triu = jnp.where(lax.broadcasted_iota(I32, (128, 128), 0) < lax.broadcasted_iota(I32, (128, 128), 1),
                     1.0, 0.0).astype(BF16)
    ones = jnp.ones((128, 128), BF16)
    carry = jnp.zeros((R, 128), F32)
    for s in range(n_pages + 1):
        x = slab(s)
        eq = x == thr
        eqb = jnp.where(eq, 1.0, 0.0).astype(BF16)
        rank = jnp.dot(eqb, triu, preferred_element_type=F32) + carry
        mb = jnp.where(x > thr, 0.0, jnp.where(eq, jnp.where(rank < need, 0.0, NEG), NEG))
        if s == n_pages:
            mb = jnp.where(new_ok, mb, NEG)
        mb_ref[:, :, s * PAGE_SIZE:(s + 1) * PAGE_SIZE] = mb.reshape(SQ, QR, PAGE_SIZE)
        carry = carry + jnp.dot(eqb, ones, preferred_element_type=F32)


def _dsa_s_select(keys, qi32, w32, ki_new_t, ksel, t_new):
    nb, _, L = keys.shape
    n_pages = L // PAGE_SIZE
    Lp = L + PAGE_SIZE
    assert nb % SQ == 0
    return pl.pallas_call(
        functools.partial(_dsa_s_select_body, n_pages=n_pages, ksel=ksel, t_new=t_new),
        grid=(nb // SQ,),
        in_specs=[pl.BlockSpec((SQ, QR, L), lambda i: (i, 0, 0)),
                  pl.BlockSpec((SQ, IDX_HEADS * QR, IDX_DIM), lambda i: (i, 0, 0)),
                  pl.BlockSpec((SQ, IDX_HEADS * QR, 128), lambda i: (i, 0, 0)),
                  pl.BlockSpec((SQ, IDX_DIM, PAGE_SIZE), lambda i: (i, 0, 0))],
        out_specs=pl.BlockSpec((SQ, QR, Lp), lambda i: (i, 0, 0)),
        out_shape=jax.ShapeDtypeStruct((nb, QR, Lp), F32),
        compiler_params=_cparams(("parallel",)), name="dsa_sample_select",
    )(keys, qi32, w32, ki_new_t)


def _dsa_s_attn_body(pt_ref, q_ref, mb_ref, mbn_ref, kvn_ref, bias_ref, *rest):
    kpages = rest[:PG]
    vpages = rest[PG:2 * PG]
    o_ref = rest[2 * PG]
    m_scr, l_scr, acc_scr = rest[2 * PG + 1:]
    j = pl.program_id(1)
    last = j == pl.num_programs(1) - 1
    scale = (B_HEAD_DIM ** -0.5) * LOG2E

    @pl.when(j == 0)
    def _init():
        m_scr[...] = jnp.full(m_scr.shape, NEG, F32)
        l_scr[...] = jnp.zeros(l_scr.shape, F32)
        acc_scr[...] = jnp.zeros(acc_scr.shape, F32)

    R = B_GROUP * QR
    nrep = B_KV_HEADS * B_GROUP
    qs = [(q_ref[g] * scale).astype(BF16) for g in range(B_KV_HEADS)]

    def near_bias(kind):
        return jnp.concatenate([bias_ref[h, kind] for h in range(B_HEADS)], axis=0)

    def update(s, vs):
        m_old = m_scr[...]
        m_new = jnp.maximum(m_old, jnp.max(s, axis=1, keepdims=True))
        alpha = jnp.exp2(m_old - m_new)
        p = jnp.exp2(s - m_new)
        l_scr[...] = alpha * l_scr[...] + jnp.sum(p, axis=1, keepdims=True)
        pb = p.astype(BF16)
        pv = jnp.concatenate([jnp.dot(pb[g * R:(g + 1) * R], vs[g], preferred_element_type=F32)
                              for g in range(B_KV_HEADS)], axis=0)
        acc_scr[...] = alpha * acc_scr[...] + pv
        m_scr[...] = m_new

    ks = [jnp.concatenate([kpages[p][pl.ds(g, PAGE_SIZE, stride=B_KV_HEADS), :].astype(BF16)
                           for p in range(PG)], axis=0) for g in range(B_KV_HEADS)]
    vs = [jnp.concatenate([vpages[p][pl.ds(g, PAGE_SIZE, stride=B_KV_HEADS), :].astype(BF16)
                           for p in range(PG)], axis=0) for g in range(B_KV_HEADS)]
    s = jnp.concatenate([lax.dot_general(qs[g], ks[g], NT_DIMS, preferred_element_type=F32)
                         for g in range(B_KV_HEADS)], axis=0)
    s = s + jnp.concatenate([mb_ref[...]] * nrep, axis=0)
    nb0 = jnp.where(last, near_bias(0), 0.0)
    s = jnp.concatenate([s[:, :(PG - 1) * PAGE_SIZE], s[:, (PG - 1) * PAGE_SIZE:] + nb0], axis=1)
    update(s, vs)

    @pl.when(last)
    def _fin():
        CW = B_KV_HEADS * B_HEAD_DIM
        kn = [kvn_ref[:, g * B_HEAD_DIM:(g + 1) * B_HEAD_DIM].astype(BF16) for g in range(B_KV_HEADS)]
        vn = [kvn_ref[:, CW + g * B_HEAD_DIM:CW + (g + 1) * B_HEAD_DIM].astype(BF16) for g in range(B_KV_HEADS)]
        sn = jnp.concatenate([lax.dot_general(qs[g], kn[g], NT_DIMS, preferred_element_type=F32)
                              for g in range(B_KV_HEADS)], axis=0)
        sn = sn + jnp.concatenate([mbn_ref[...]] * nrep, axis=0) + near_bias(1)
        update(sn, vn)
        out = acc_scr[...] * (1.0 / l_scr[...])
        for g in range(B_KV_HEADS):
            o_ref[g] = out[g * R:(g + 1) * R]


def _dsa_s_attn(page_table, q32, mb, kv_new, bias_s, ck_rows, cv_rows):
    nb, n_pages = page_table.shape
    assert n_pages % PG == 0
    nj = n_pages // PG
    R = B_GROUP * QR
    PR = PAGE_SIZE * B_KV_HEADS

    def page_spec(p):
        return pl.BlockSpec((PR, B_HEAD_DIM), lambda b, j, pt: (pt[b, j * PG + p], 0))

    grid_spec = pltpu.PrefetchScalarGridSpec(
        num_scalar_prefetch=1, grid=(nb, nj),
        in_specs=[pl.BlockSpec((None, B_KV_HEADS, R, B_HEAD_DIM), lambda b, j, pt: (b, 0, 0, 0)),
                  pl.BlockSpec((None, QR, PG * PAGE_SIZE), lambda b, j, pt: (b, 0, j)),
                  pl.BlockSpec((None, QR, PAGE_SIZE), lambda b, j, pt: (b, 0, n_pages)),
                  pl.BlockSpec((None, PAGE_SIZE, 2 * B_KV_HEADS * B_HEAD_DIM), lambda b, j, pt: (b, 0, 0)),
                  pl.BlockSpec(bias_s.shape, lambda b, j, pt: (0, 0, 0, 0))]
                 + [page_spec(p) for p in range(PG)] + [page_spec(p) for p in range(PG)],
        out_specs=pl.BlockSpec((None, B_KV_HEADS, R, B_HEAD_DIM), lambda b, j, pt: (b, 0, 0, 0)),
        scratch_shapes=[pltpu.VMEM((B_KV_HEADS * R, 1), F32), pltpu.VMEM((B_KV_HEADS * R, 1), F32),
                        pltpu.VMEM((B_KV_HEADS * R, B_HEAD_DIM), F32)])
    return pl.pallas_call(
        _dsa_s_attn_body, grid_spec=grid_spec,
        out_shape=jax.ShapeDtypeStruct((nb, B_KV_HEADS, R, B_HEAD_DIM), F32),
        compiler_params=_cparams(("parallel", "arbitrary")), name="dsa_sample_attn",
    )(page_table, q32, mb, mb, kv_new, bias_s, *([ck_rows] * PG), *([cv_rows] * PG))


def _layer_norm(r, g, b):
    mu = jnp.mean(r, axis=-1, keepdims=True)
    d = r - mu
    var = jnp.mean(d * d, axis=-1, keepdims=True)
    return d * lax.rsqrt(var + LN_EPS) * g + b


def _merge_body(oa_ref, ob_ref, gates_ref, x_ref, g1_ref, sh2_ref, sc2_ref, w_ref, lng_ref, lnb_ref,
                x1_ref, h2_ref, *, alpha):
    D = oa_ref.shape[1]
    merged = _sigmoid(gates_ref[:, 0:D]) * oa_ref[...] + _sigmoid(gates_ref[:, D:2 * D]) * ob_ref[...]
    y = jnp.dot(merged.astype(BF16), w_ref[...], preferred_element_type=F32)
    x1 = _layer_norm(alpha * x_ref[...] + g1_ref[...] * y, lng_ref[...], lnb_ref[...])
    x1_ref[...] = x1
    h2_ref[...] = (x1 * (1.0 + sc2_ref[...]) + sh2_ref[...]).astype(BF16)


def _merge(oa, ob, z, x2d, mod, rows_per_mod, w_out_bf, ln_g, ln_b, alpha, tm):
    N, D = x2d.shape
    row = lambda i: (i, 0)
    const = lambda i: (0, 0)
    return pl.pallas_call(
        functools.partial(_merge_body, alpha=alpha), grid=(N // tm,),
        in_specs=[pl.BlockSpec((tm, D), row), pl.BlockSpec((tm, D), row),
                  pl.BlockSpec((tm, 2 * D), lambda i: (i, Z_GATES // (2 * D))),
                  pl.BlockSpec((tm, D), row),
                  _mod_spec(mod, 2, tm, rows_per_mod, D, 1),
                  _mod_spec(mod, 3, tm, rows_per_mod, D, 1),
                  _mod_spec(mod, 4, tm, rows_per_mod, D, 1),
                  pl.BlockSpec((D, D), const), pl.BlockSpec((1, D), const), pl.BlockSpec((1, D), const)],
        out_specs=[pl.BlockSpec((tm, D), row), pl.BlockSpec((tm, D), row)],
        out_shape=[jax.ShapeDtypeStruct((N, D), F32), jax.ShapeDtypeStruct((N, D), BF16)],
        compiler_params=_cparams(("parallel",)), name="merge_out",
    )(oa, ob, z, x2d, mod, mod, mod, w_out_bf, ln_g, ln_b)


def _top_values(x, n, with_rank=False):
    out = []
    rank = jnp.full(x.shape, float(n), F32) if with_rank else None
    for r in range(n):
        m = jnp.max(x, axis=0, keepdims=True)
        out.append(m)
        if with_rank:
            rank = jnp.where(x == m, float(r), rank)
        if r < n - 1:
            x = jnp.where(x == m, -jnp.inf, x)
    return (out, rank) if with_rank else out


_PEER_PAIRS = [(a, b) for a in range(PEER_TOPK) for b in range(PEER_TOPK) if (a + 1) * (b + 1) <= PEER_TOPK]


def _peer_body(h2_ref, x1_ref, g2_ref, wq_ref, sk_ref, u0_ref, un_ref, vt_ref, lng_ref, lnb_ref, o_ref,
               np_scr, r2_scr, e1_scr, e2_scr, acc_scr, act_a, act_b, w_ab, *, tn, ec, nchunks, alpha):
    e = pl.program_id(1)
    ne = pl.num_programs(1) - 1
    n_i1 = ec // N_KEYS

    @pl.when(e == 0)
    def _select():
        q = jnp.dot(h2_ref[...], wq_ref[...], preferred_element_type=F32)
        for h in range(PEER_HEADS):
            sc = []
            for c in range(2):
                qhc = q[:, (2 * h + c) * N_KEYS:(2 * h + c + 1) * N_KEYS].astype(BF16)
                sc.append(lax.dot_general(sk_ref[c], qhc, NT_DIMS, preferred_element_type=F32))
            s1, s2 = sc
            t1 = _top_values(s1, PEER_TOPK)
            t2, rank2 = _top_values(s2, PEER_TOPK, with_rank=True)
            cand = jnp.concatenate([t1[a] + t2[b] for a, b in _PEER_PAIRS], axis=0)
            thr = _top_values(cand, PEER_TOPK)[-1]
            mx = t1[0] + t2[0]
            zsum = jnp.sum(jnp.where(cand >= thr, jnp.exp(cand - mx), 0.0), axis=0, keepdims=True)
            passed = jnp.where(cand >= thr, 1.0, 0.0)
            npass = jnp.zeros(s1.shape, F32)
            for a in range(PEER_TOPK):
                rows = [i for i, (pa, _) in enumerate(_PEER_PAIRS) if pa == a]
                np_a = passed[rows[0]:rows[0] + 1]
                for i in rows[1:]:
                    np_a = np_a + passed[i:i + 1]
                npass = jnp.where(s1 == t1[a], np_a, npass)
            np_scr[h] = npass
            r2_scr[h] = rank2.astype(BF16)
            e1_scr[h] = jnp.exp(s1 - t1[0]) * (1.0 / zsum)
            e2_scr[h] = jnp.exp(s2 - t2[0]).astype(BF16)
        act_a[...] = lax.dot_general(u0_ref[...], h2_ref[...], NT_DIMS, preferred_element_type=F32)
        w_ab[1] = jnp.zeros(w_ab.shape[1:], BF16)
        acc_scr[...] = jnp.zeros(acc_scr.shape, F32)

    PK = 16

    def chunk(act_cur, act_nxt, w_cur, w_prev):
        acc_scr[...] += jnp.dot(vt_ref[...], w_prev[...], preferred_element_type=F32)

        act_nxt[...] = lax.dot_general(un_ref[...], h2_ref[...], NT_DIMS, preferred_element_type=F32)
        for jj in range(n_i1):
            i1 = e * n_i1 + jj
            a = act_cur[jj * N_KEYS:(jj + 1) * N_KEYS, :]
            gsum = jnp.zeros((N_KEYS // PK, PK, tn), BF16)
            for h in range(PEER_HEADS):
                npass_row = jnp.broadcast_to(np_scr[h, pl.ds(i1, 1), :], (PK, tn)).astype(BF16)
                e1_row = jnp.broadcast_to(e1_scr[h, pl.ds(i1, 1), :], (PK, tn)).astype(BF16)
                sel = r2_scr[h].reshape(N_KEYS // PK, PK, tn) < npass_row
                e2 = e2_scr[h].reshape(N_KEYS // PK, PK, tn)
                gsum = gsum + jnp.where(sel, e2, jnp.zeros_like(e2)) * e1_row
            gelu = 0.5 * a * (1.0 + lax.erf(a * (2.0 ** -0.5)))
            w_cur[jj * N_KEYS:(jj + 1) * N_KEYS, :] = gsum.reshape(N_KEYS, tn) * gelu.astype(BF16)

    @pl.when((e < ne) & (e % 2 == 0))
    def _even():
        chunk(act_a, act_b, w_ab.at[e % 2], w_ab.at[1 - e % 2])

    @pl.when((e < ne) & (e % 2 == 1))
    def _odd():
        chunk(act_b, act_a, w_ab.at[e % 2], w_ab.at[1 - e % 2])

    @pl.when(e == ne)
    def _fin():
        acc = acc_scr[...] + jnp.dot(vt_ref[...], w_ab[(nchunks - 1) % 2], preferred_element_type=F32)
        y = acc.T
        o_ref[...] = _layer_norm(alpha * x1_ref[...] + g2_ref[...] * y, lng_ref[...], lnb_ref[...])


def _peer(h2, x1, mod, rows_per_mod, wq_bf, sk_bf, u_bf, vt_bf, ln_g, ln_b, alpha, tn, ec):
    N, D = x1.shape
    E = u_bf.shape[0]
    ne = E // ec
    row = lambda i, e: (i, 0)
    const2 = lambda i, e: (0, 0)
    if mod.ndim == 3:
        g2_spec = pl.BlockSpec((None, 1, D), lambda i, e: ((i * tn) // rows_per_mod, 0, 5))
    else:
        g2_spec = pl.BlockSpec((tn, D), lambda i, e: (i, 5))
    return pl.pallas_call(
        functools.partial(_peer_body, tn=tn, ec=ec, nchunks=ne, alpha=alpha), grid=(N // tn, ne + 1),
        in_specs=[pl.BlockSpec((tn, D), row), pl.BlockSpec((tn, D), row), g2_spec,
                  _resident(wq_bf.shape, const2),
                  _resident(sk_bf.shape, lambda i, e: (0, 0, 0)),
                  _resident((ec, D), const2),
                  pl.BlockSpec((ec, D), lambda i, e: (jnp.minimum(e + 1, ne - 1), 0)),
                  pl.BlockSpec((D, ec), lambda i, e: (0, jnp.maximum(e - 1, 0))),
                  pl.BlockSpec((1, D), const2), pl.BlockSpec((1, D), const2)],
        out_specs=pl.BlockSpec((tn, D), row),
        out_shape=jax.ShapeDtypeStruct((N, D), F32),
        scratch_shapes=[pltpu.VMEM((PEER_HEADS, N_KEYS, tn), F32), pltpu.VMEM((PEER_HEADS, N_KEYS, tn), BF16),
                        pltpu.VMEM((PEER_HEADS, N_KEYS, tn), F32), pltpu.VMEM((PEER_HEADS, N_KEYS, tn), BF16),
                        pltpu.VMEM((D, tn), F32),
                        pltpu.VMEM((ec, tn), F32), pltpu.VMEM((ec, tn), F32), pltpu.VMEM((2, ec, tn), BF16)],
        compiler_params=_cparams(("parallel", "arbitrary")), name="peer",
    )(h2, x1, mod, wq_bf, sk_bf, u_bf, u_bf, vt_bf, ln_g, ln_b)


def _pack_w_in(w_in):
    D = w_in.shape[0]
    idx_w = w_in[:, 5632:5956]
    pad = jnp.zeros((D, Z_GATES - Z_QI - idx_w.shape[1]), w_in.dtype)
    return jnp.concatenate([w_in[:, :5632], idx_w, pad, w_in[:, 5956:]], axis=1).astype(BF16)


def _prompt_buckets(tq):
    i = np.arange(tq)[:, None]
    j = np.arange(tq)[None, :]
    return np.stack([_t5_bucket_np(np.maximum(i - j, 0)), _t5_bucket_np(tq + i - j)])


def _sample_buckets():
    t = np.arange(QR)[:, None]
    o = np.arange(PAGE_SIZE)[None, :]
    return np.stack([_t5_bucket_np(PAGE_SIZE + t - o), _t5_bucket_np(np.maximum(t - o, 0))])


def _layer_tail(oa, ob, z, x2d, mod, rows_per_mod, wts, alpha, tm, tn_peer):
    x1, h2 = _merge(oa, ob, z, x2d, mod, rows_per_mod, wts["w_out"], wts["ln1_g"], wts["ln1_b"], alpha, tm)
    return _peer(h2, x1, mod, rows_per_mod, wts["w_peer_q"], wts["sub_keys"], wts["peer_u"], wts["peer_vt"],
                 wts["ln2_g"], wts["ln2_b"], alpha, tn_peer, 1024)


def kernel(x_prompt, x_sample, cache_k, cache_v, cache_kidx, state_hgrn, page_table, c_prompt, c_sample,
           w_ada, b_ada, w_in, lb_param, a_gnorm, t5_table, w_out, ln1_g, ln1_b, w_peer_q, peer_sub_keys,
           peer_u, peer_v, ln2_g, ln2_b):
    depth = w_ada.shape[0]
    assert depth == 1 and lb_param.shape[0] == 2
    alpha = (2.0 * depth) ** 0.25
    B, T, D = x_prompt.shape
    NB, TS, _ = x_sample.shape
    n_pages = page_table.shape[1]
    past_len = n_pages * PAGE_SIZE

    wts = dict(
        w_out=w_out[0].astype(BF16), ln1_g=ln1_g, ln1_b=ln1_b, ln2_g=ln2_g, ln2_b=ln2_b,
        w_peer_q=w_peer_q[0].astype(BF16), sub_keys=peer_sub_keys[0].astype(BF16),
        peer_u=peer_u[0].astype(BF16), peer_vt=peer_v[0].T.astype(BF16))
    w_in_bf = _pack_w_in(w_in[0])
    gnorm = a_gnorm.reshape(1, A_DK)

    mod = _ada(jnp.concatenate([c_prompt, c_sample], axis=0), w_ada[0].astype(BF16), b_ada)
    mod_p = mod[:B].reshape(B, 1, 6 * D)
    mod_s = jnp.repeat(mod[B:], TS, axis=0)

    xp = x_prompt.reshape(B * T, D)
    zp, kp_rows, vp_rows = _inproj(xp, mod_p, T, w_in_bf, 256, 1024)
    oa_p, s_p = _hgrn(zp, jnp.zeros((B, A_HEADS, A_DK, A_DK), F32), lb_param, gnorm, B, T, 256, A_CHUNK, 256, 8)
    tq = 256
    bias_p = _bias_tiles(t5_table, _prompt_buckets(tq))
    bias_g = bias_p.reshape(B_KV_HEADS, B_GROUP, 2, tq, tq).transpose(0, 2, 1, 3, 4)
    bias_g = bias_g.reshape(B_KV_HEADS, 2, B_GROUP * tq, tq)
    bias_g = jnp.concatenate([bias_g, jnp.zeros((B_KV_HEADS, 1, B_GROUP * tq, tq), F32)], axis=1)
    ob_p = _dsa_prompt(zp, bias_g, B, T, tq)
    y_p = _layer_tail(oa_p, ob_p, zp, xp, mod_p, T, wts, alpha, 256, 512)

    xs = x_sample.reshape(NB * TS, D)
    zs, ks_rows, vs_rows = _inproj(xs, mod_s, 1, w_in_bf, min(256, NB * TS), 1024)
    zs3 = zs.reshape(NB, TS, Z_WIDTH)
    zs_pad = jnp.pad(zs3, ((0, 0), (0, QR - TS), (0, 0))).reshape(NB * QR, Z_WIDTH)
    assert TS <= QR
    oa_s8, s_s = _hgrn(zs_pad, state_hgrn[0], lb_param, gnorm, NB, QR, QR, QR, TS, A_HEADS)
    oa_s = oa_s8.reshape(NB, QR, D)[:, :TS].reshape(NB * TS, D)

    qi_s = zs3[:, :, Z_QI:Z_QI + IDX_HEADS * IDX_DIM].reshape(NB, TS, IDX_HEADS, IDX_DIM)
    qi32 = jnp.pad(qi_s.transpose(0, 2, 1, 3), ((0, 0), (0, 0), (0, QR - TS), (0, 0)))
    qi32 = qi32.reshape(NB, IDX_HEADS * QR, IDX_DIM).astype(BF16)
    wi_s = zs3[:, :, Z_WI:Z_WI + IDX_HEADS].transpose(0, 2, 1)
    w32 = jnp.pad(wi_s, ((0, 0), (0, 0), (0, QR - TS))).reshape(NB, IDX_HEADS * QR, 1)
    w32 = jnp.broadcast_to(w32, (NB, IDX_HEADS * QR, 128))
    ki_new = jnp.pad(zs3[:, :, Z_KI:Z_KI + IDX_DIM], ((0, 0), (0, PAGE_SIZE - TS), (0, 0))).astype(BF16)
    ki_new_t = jnp.swapaxes(ki_new, 1, 2)
    kv_new = jnp.pad(zs3[:, :, Z_KB:Z_KB + 512], ((0, 0), (0, PAGE_SIZE - TS), (0, 0)))
    q_s = zs3[:, :, Z_QB:Z_QB + 1024].reshape(NB, TS, B_KV_HEADS, B_GROUP, B_HEAD_DIM)
    q32 = jnp.pad(q_s.transpose(0, 2, 3, 1, 4), ((0, 0), (0, 0), (0, 0), (0, QR - TS), (0, 0)))
    q32 = q32.reshape(NB, B_KV_HEADS, B_GROUP * QR, B_HEAD_DIM)

    ksel = min(TOPK_MAX, (past_len + TS) // 4)
    kidx_t = jnp.swapaxes(cache_kidx[0], 1, 2)
    keys_s = _dsa_s_scores(page_table, qi32, w32, kidx_t)
    mb_s = _dsa_s_select(keys_s, qi32, w32, ki_new_t, ksel, TS)
    bias_s = _bias_tiles(t5_table, _sample_buckets())
    n_pool = cache_k.shape[1]
    ck = cache_k.reshape(n_pool * PAGE_SIZE * B_KV_HEADS, B_HEAD_DIM)
    cv = cache_v.reshape(n_pool * PAGE_SIZE * B_KV_HEADS, B_HEAD_DIM)
    o32 = _dsa_s_attn(page_table, q32, mb_s, kv_new, bias_s, ck, cv)
    ob_s = o32.reshape(NB, B_KV_HEADS, B_GROUP, QR, B_HEAD_DIM)[:, :, :, :TS]
    ob_s = ob_s.transpose(0, 3, 1, 2, 4).reshape(NB * TS, D)
    y_s = _layer_tail(oa_s, ob_s, zs, xs, mod_s, 1, wts, alpha, NB * TS, NB * TS)

    def kv_out(z, k_rows, v_rows, nb, t):
        k = k_rows.reshape(1, nb, t, B_KV_HEADS, B_HEAD_DIM)
        v = v_rows.reshape(1, nb, t, B_KV_HEADS, B_HEAD_DIM)
        ki = z[:, Z_KI:Z_KI + IDX_DIM].reshape(1, nb, t, IDX_DIM)
        return k, v, ki

    kp, vp, ip = kv_out(zp, kp_rows, vp_rows, B, T)
    ks, vs, is_ = kv_out(zs, ks_rows, vs_rows, NB, TS)
    return (y_p.reshape(B, T, D), y_s.reshape(NB, TS, D), kp, vp, ip, s_p[None],
            ks, vs, is_, s_s[None])
```

```python
import functools
import math

import numpy as np
import jax
import jax.numpy as jnp
from jax import lax
from jax.experimental import pallas as pl
from jax.experimental.pallas import tpu as pltpu

F32 = jnp.float32
BF16 = jnp.bfloat16
I32 = jnp.int32
I16 = jnp.int16

A_HEADS = 8
A_DK = 128
A_CHUNK = 32
B_HEADS = 8
B_KV_HEADS = 2
B_GROUP = 4
B_HEAD_DIM = 128
IDX_HEADS = 4
IDX_DIM = 64
TOPK_MAX = 256
N_BUCKETS = 32
MAX_DISTANCE = 128
PEER_HEADS = 8
N_KEYS = 128
PEER_TOPK = 16
PAGE_SIZE = 128
LN_EPS = 1e-5
RMS_EPS = 1e-6

Z_A = 0
Z_QB = 4096
Z_KB = 5120
Z_VB = 5376
Z_QI = 5632
Z_KI = 5888
Z_WI = 5952
Z_GATES = 6144
Z_WIDTH = 8192

NEG = -1e30
INT_MIN = -2 ** 31
I16_MIN = -2 ** 15
LOG2E = 1.4426950408889634
NT_DIMS = (((1,), (1,)), ((), ()))
VMEM_LIMIT = 56 * 1024 * 1024


def _cparams(sem):
    return pltpu.CompilerParams(dimension_semantics=sem, vmem_limit_bytes=VMEM_LIMIT)


def _resident(shape, index_map):
    return pl.BlockSpec(shape, index_map, pipeline_mode=pl.Buffered(1))


def _sigmoid(x):
    return jax.nn.sigmoid(x)


def _ada_body(c_ref, w_ref, b_ref, o_ref):
    c = c_ref[...]
    s = c * _sigmoid(c)
    o_ref[...] = jnp.dot(s.astype(BF16), w_ref[...], preferred_element_type=F32) + b_ref[...]


def _ada(c, w_bf, b):
    R, D = c.shape
    N = w_bf.shape[1]
    tn = N // 4
    return pl.pallas_call(
        _ada_body, grid=(N // tn,),
        in_specs=[pl.BlockSpec((R, D), lambda j: (0, 0)),
                  pl.BlockSpec((D, tn), lambda j: (0, j)),
                  pl.BlockSpec((1, tn), lambda j: (0, j))],
        out_specs=pl.BlockSpec((R, tn), lambda j: (0, j)),
        out_shape=jax.ShapeDtypeStruct((R, N), F32),
        compiler_params=_cparams(("parallel",)), name="ada")(c, w_bf, b)


def _mod_spec(mod, blk, tm, rows_per_mod, D, nidx):
    if mod.ndim == 3:
        if nidx == 1:
            return pl.BlockSpec((None, 1, D), lambda i: ((i * tm) // rows_per_mod, 0, blk))
        return pl.BlockSpec((None, 1, D), lambda i, j: ((i * tm) // rows_per_mod, 0, blk))
    if nidx == 1:
        return pl.BlockSpec((tm, D), lambda i: (i, blk))
    return pl.BlockSpec((tm, D), lambda i, j: (i, blk))


def _inproj_body(x_ref, sh_ref, sc_ref, w_ref, o_ref, k_ref, v_ref, *, tn):
    tm = x_ref.shape[0]
    h = (x_ref[...] * (1.0 + sc_ref[...]) + sh_ref[...]).astype(BF16)
    for j in range(o_ref.shape[1] // tn):
        zt = jnp.dot(h, w_ref[:, j * tn:(j + 1) * tn], preferred_element_type=F32)
        o_ref[:, j * tn:(j + 1) * tn] = zt
        if j * tn <= Z_KB and Z_VB + B_KV_HEADS * B_HEAD_DIM <= (j + 1) * tn:
            for g in range(B_KV_HEADS):
                ko = Z_KB - j * tn + g * B_HEAD_DIM
                vo = Z_VB - j * tn + g * B_HEAD_DIM
                k_ref[pl.ds(g, tm, stride=B_KV_HEADS), :] = zt[:, ko:ko + B_HEAD_DIM]
                v_ref[pl.ds(g, tm, stride=B_KV_HEADS), :] = zt[:, vo:vo + B_HEAD_DIM]


def _inproj(x2d, mod, rows_per_mod, w_bf, tm, tn):
    N, D = x2d.shape
    W = w_bf.shape[1]
    kv_spec = pl.BlockSpec((tm * B_KV_HEADS, B_HEAD_DIM), lambda i: (i, 0))
    kv_shape = jax.ShapeDtypeStruct((N * B_KV_HEADS, B_HEAD_DIM), F32)
    return pl.pallas_call(
        functools.partial(_inproj_body, tn=tn), grid=(N // tm,),
        in_specs=[pl.BlockSpec((tm, D), lambda i: (i, 0)),
                  _mod_spec(mod, 0, tm, rows_per_mod, D, 1),
                  _mod_spec(mod, 1, tm, rows_per_mod, D, 1),
                  _resident((D, W), lambda i: (0, 0))],
        out_specs=[pl.BlockSpec((tm, W), lambda i: (i, 0)), kv_spec, kv_spec],
        out_shape=[jax.ShapeDtypeStruct((N, W), F32), kv_shape, kv_shape],
        compiler_params=_cparams(("parallel",)), name="inproj")(x2d, mod, mod, w_bf)


def _hgrn_body(q_ref, f_ref, i_ref, g_ref, lbp_ref, gn_ref, s0_ref, o_ref, sout_ref, st_scr,
               *, C, Tb, t_valid, HB):
    tb = pl.program_id(2)
    shift = int(math.log2(C))
    nchunk = Tb // C
    r = lax.broadcasted_iota(I32, (Tb, Tb), 0)
    c = lax.broadcasted_iota(I32, (Tb, Tb), 1)
    tril = (jnp.right_shift(r, shift) == jnp.right_shift(c, shift)) & (c <= r)
    trilf = jnp.where(tril, 1.0, 0.0)
    Tp = max(Tb, 128)
    colchunk = jnp.right_shift(lax.broadcasted_iota(I32, (A_DK, Tp), 1), shift)

    lbp = lbp_ref[...]
    mx = jnp.max(lbp, axis=0, keepdims=True)
    e = jnp.exp(lbp - mx)
    lb_all = e[0:1, :] / jnp.sum(e, axis=0, keepdims=True)

    @pl.when(tb == 0)
    def _init():
        for hh in range(HB):
            st_scr[hh] = s0_ref[hh].T

    W = HB * A_DK
    f = lb_all + (1.0 - lb_all) * _sigmoid(f_ref[...])
    lf = jnp.log(f)
    k = 1.0 - f
    if t_valid < Tb:
        ok = lax.broadcasted_iota(I32, (Tb, W), 0) < t_valid
        lf = jnp.where(ok, lf, 0.0)
        k = jnp.where(ok, k, 0.0)
    bcum = jnp.dot(trilf, lf, precision=lax.Precision.HIGHEST,
                   preferred_element_type=F32)
    tot = jnp.concatenate(
        [jnp.broadcast_to(bcum[(j + 1) * C - 1:(j + 1) * C, :], (C, W)) for j in range(nchunk)], axis=0)
    qa = q_ref[...]
    qe_b = ((qa * _sigmoid(qa)) * jnp.exp(bcum)).astype(BF16)
    ke_b = (k * jnp.exp(-bcum)).astype(BF16)
    kd = k * jnp.exp(tot - bcum)
    v = i_ref[...]
    v_b = v.astype(BF16)
    if Tb < 128:
        zpad = jnp.zeros((128 - Tb, W), F32)
        v_p = jnp.concatenate([v, zpad], axis=0)
        kd_p = jnp.concatenate([kd, zpad], axis=0)
    else:
        v_p, kd_p = v, kd
    kd_b = kd_p.astype(BF16)
    heads = [(hh * A_DK, (hh + 1) * A_DK) for hh in range(HB)]

    scores = [jnp.where(tril, lax.dot_general(qe_b[:, lo:hi], ke_b[:, lo:hi], NT_DIMS,
                                              preferred_element_type=F32), 0.0).astype(BF16) for lo, hi in heads]
    o = [jnp.dot(scores[hh], v_b[:, lo:hi], preferred_element_type=F32) for hh, (lo, hi) in enumerate(heads)]
    vT = [v_p[:, lo:hi].T for lo, hi in heads]
    ST = [st_scr[hh] for hh in range(HB)]
    o_inter = [[] for _ in range(HB)]
    for j in range(nchunk):
        for hh, (lo, hi) in enumerate(heads):
            o_inter[hh].append(lax.dot_general(qe_b[j * C:(j + 1) * C, lo:hi], ST[hh].astype(BF16), NT_DIMS,
                                               preferred_element_type=F32))
            d_c = jnp.exp(tot[j * C:j * C + 1, lo:hi])
            vTm = jnp.where(colchunk == j, vT[hh], 0.0) if nchunk > 1 else vT[hh]
            ST[hh] = ST[hh] * d_c + jnp.dot(vTm.astype(BF16), kd_b[:, lo:hi], preferred_element_type=F32)
    gn = gn_ref[...]
    ga = g_ref[...]
    gate = ga * _sigmoid(ga)
    for hh, (lo, hi) in enumerate(heads):
        st_scr[hh] = ST[hh]
        oh = o[hh] + jnp.concatenate(o_inter[hh], axis=0)
        ms = jnp.mean(oh * oh, axis=-1, keepdims=True)
        o_ref[:, lo:hi] = oh * lax.rsqrt(ms + RMS_EPS) * gn * gate[:, lo:hi]

    @pl.when(tb == pl.num_programs(2) - 1)
    def _fin():
        for hh in range(HB):
            sout_ref[hh] = st_scr[hh].T


def _hgrn(z, s0, lb_param, gnorm, B, T, Tb, C, t_valid, HB):
    nt = T // Tb
    H = A_HEADS
    W = HB * A_DK
    ng = H // HB

    def col(group):
        return pl.BlockSpec((Tb, W), lambda b, h, t: (b * nt + t, group * ng + h))

    return pl.pallas_call(
        functools.partial(_hgrn_body, C=C, Tb=Tb, t_valid=t_valid, HB=HB),
        grid=(B, ng, nt),
        in_specs=[col(0), col(1), col(2), col(3),
                  pl.BlockSpec((2, W), lambda b, h, t: (0, h)),
                  pl.BlockSpec((1, A_DK), lambda b, h, t: (0, 0)),
                  pl.BlockSpec((None, HB, A_DK, A_DK), lambda b, h, t: (b, h, 0, 0))],
        out_specs=[pl.BlockSpec((Tb, W), lambda b, h, t: (b * nt + t, h)),
                   pl.BlockSpec((None, HB, A_DK, A_DK), lambda b, h, t: (b, h, 0, 0))],
        out_shape=[jax.ShapeDtypeStruct((B * T, H * A_DK), F32),
                   jax.ShapeDtypeStruct((B, H, A_DK, A_DK), F32)],
        scratch_shapes=[pltpu.VMEM((HB, A_DK, A_DK), F32)],
        compiler_params=_cparams(("parallel", "parallel", "arbitrary")), name="hgrn",
    )(z, z, z, z, lb_param, gnorm, s0)


def _t5_bucket_np(rel):
    rel = np.asarray(rel, np.int64)
    max_exact = N_BUCKETS // 2
    relf = np.maximum(rel, 1).astype(np.float64)
    large = max_exact + (np.log(relf / max_exact) / math.log(MAX_DISTANCE / max_exact)
                         * (N_BUCKETS - max_exact)).astype(np.int64)
    large = np.minimum(large, N_BUCKETS - 1)
    return np.where(rel < max_exact, rel, large).astype(np.int32)


def _bias_body(tab_ref, bkt_ref, o_ref):
    h = pl.program_id(0)
    b = bkt_ref[...]
    acc = jnp.zeros(b.shape, F32)
    for i in range(N_BUCKETS):
        acc = jnp.where(b == i, tab_ref[i, h], acc)
    o_ref[...] = (acc - tab_ref[N_BUCKETS - 1, h]) * LOG2E


def _bias_tiles(t5_table, buckets):
    K, R, Cc = buckets.shape
    return pl.pallas_call(
        _bias_body, grid=(B_HEADS, K),
        in_specs=[pl.BlockSpec(memory_space=pltpu.SMEM),
                  pl.BlockSpec((None, R, Cc), lambda h, k: (k, 0, 0))],
        out_specs=pl.BlockSpec((None, None, R, Cc), lambda h, k: (h, k, 0, 0)),
        out_shape=jax.ShapeDtypeStruct((B_HEADS, K, R, Cc), F32),
        compiler_params=_cparams(("parallel", "parallel")), name="t5_bias")(t5_table, jnp.asarray(buckets))


def _ordered_key(score):
    score = jnp.where(score == 0.0, 0.0, score)
    bits = pltpu.bitcast(score, I32)
    return jnp.where(bits < 0, jnp.bitwise_xor(bits, jnp.int32(0x7FFFFFFF)), bits)


def _bcast_lanes(col, n):
    return jnp.broadcast_to(col, (col.shape[0], n))


def _dsa_prompt_body(q_ref, kv_ref, idxq_ref, ki_ref, bias_ref, o_ref,
                     key_scr, mb_scr, m_scr, l_scr, acc_scr, s_scr, hi_scr, lo_scr, *, tq, ksel):
    qi = pl.program_id(1)
    nvalid = qi + 1
    npair = (nvalid + 1) // 2
    tw = 2 * tq

    idxq = idxq_ref[...]
    assert (IDX_DIM ** -0.5) == 0.125 and (IDX_HEADS ** -0.5) == 0.5
    q_idx = [(idxq[:, h * IDX_DIM:(h + 1) * IDX_DIM] * (IDX_DIM ** -0.5)).astype(BF16) for h in range(IDX_HEADS)]
    w_t = idxq[:, Z_KI - Z_QI:Z_KI - Z_QI + 128].T
    w_rows = [w_t[Z_WI - Z_KI + h:Z_WI - Z_KI + h + 1, :] * (IDX_HEADS ** -0.5) for h in range(IDX_HEADS)]
    key_i = lax.broadcasted_iota(I32, (tq, tq), 0)
    qry_t = qi * tq + lax.broadcasted_iota(I32, (tq, tq), 1)

    def idx_chunk(c, carry):
        off = pl.multiple_of(c * tq, tq)
        kic = ki_ref[pl.ds(off, tq), 0:IDX_DIM].astype(BF16)
        acc = jnp.zeros((tq, tq), F32)
        for h in range(IDX_HEADS):
            s = lax.dot_general(kic, q_idx[h], NT_DIMS, preferred_element_type=F32)
            acc = acc + jnp.maximum(s, 0.0) * w_rows[h]
        score = jnp.where(off + key_i <= qry_t, acc, -jnp.inf)
        key = _ordered_key(score)
        key_scr[pl.ds(off, tq), :] = key
        hi_scr[pl.ds(off, tq), :] = jnp.right_shift(key, 16).astype(I16)
        lo_scr[pl.ds(off, tq), :] = (jnp.bitwise_and(key, 0xFFFF) - 32768).astype(I16)
        return carry

    lax.fori_loop(0, nvalid, idx_chunk, 0)

    @pl.when(nvalid % 2 == 1)
    def _pad():
        off = pl.multiple_of(nvalid * tq, tq)
        key_scr[pl.ds(off, tq), :] = jnp.full((tq, tq), INT_MIN, I32)
        hi_scr[pl.ds(off, tq), :] = jnp.full((tq, tq), I16_MIN, I16)
        lo_scr[pl.ds(off, tq), :] = jnp.full((tq, tq), I16_MIN, I16)
        mb_scr[:, pl.ds(off, tq)] = jnp.full((tq, tq), NEG, F32)

    NACC = 8
    PK = 16

    def tile16(v32):
        return jnp.broadcast_to(v32, (PK, tq)).astype(I16)

    def count16(ref, pred_fn):
        def body(c, cnt):
            off = pl.multiple_of(c * tw, tw)
            x = ref[pl.ds(off, tw), :].reshape(tw // (PK * NACC), NACC, PK, tq)
            one = jnp.ones(x.shape, I16)
            hit = jnp.where(pred_fn(x), one, jnp.zeros_like(one))
            for gi in range(hit.shape[0]):
                cnt = cnt + hit[gi]
            return cnt
        cnt = lax.fori_loop(0, npair, body, jnp.zeros((NACC, PK, tq), I16))
        return jnp.sum(jnp.sum(cnt.astype(F32), axis=0), axis=0, keepdims=True)

    def search16(ref, target):
        def bit_step(it, v):
            cand = v + jnp.left_shift(jnp.int32(1), 15 - it)
            c16 = tile16(cand)
            return jnp.where(count16(ref, lambda x: x >= c16) >= target, cand, v)
        return lax.fori_loop(0, 16, bit_step, jnp.full((1, tq), I16_MIN, I32))

    hi_k = search16(hi_scr, ksel)
    hi_t = tile16(hi_k)
    n_above = count16(hi_scr, lambda x: x > hi_t)

    def keep_lo(c, carry):
        off = pl.multiple_of(c * tw, tw)
        h = hi_scr[pl.ds(off, tw), :].reshape(tw // PK, PK, tq)
        l = lo_scr[pl.ds(off, tw), :].reshape(tw // PK, PK, tq)
        lo_scr[pl.ds(off, tw), :] = jnp.where(h == hi_t, l, jnp.full(l.shape, I16_MIN, I16)).reshape(tw, tq)
        return carry

    lax.fori_loop(0, npair, keep_lo, 0)
    lo_k = search16(lo_scr, ksel - n_above)
    lo_t = tile16(lo_k)
    thr = hi_k * 65536 + (lo_k + 32768)
    need = ksel - n_above - count16(lo_scr, lambda x: x > lo_t)
    tril = jnp.where(lax.broadcasted_iota(I32, (tq, tq), 1) < key_i, 1.0, 0.0).astype(BF16)

    def mask_chunk(c, carry):
        off = pl.multiple_of(c * tq, tq)
        x = key_scr[pl.ds(off, tq), :]
        eq = x == thr
        eqf = jnp.where(eq, 1.0, 0.0)
        rank = jnp.dot(tril, eqf.astype(BF16), preferred_element_type=F32) + carry
        mb = jnp.where(x > thr, 0.0, jnp.where(eq, jnp.where(rank < need, 0.0, NEG), NEG))
        mb = jnp.where(off + key_i <= qry_t, mb, NEG)
        mb_scr[:, pl.ds(off, tq)] = mb.T
        return carry + jnp.sum(eqf, axis=0, keepdims=True)

    lax.fori_loop(0, nvalid, mask_chunk, jnp.zeros((1, tq), F32))

    scale = (B_HEAD_DIM ** -0.5) * LOG2E
    nsl = tw // 128

    def bias_kind(c):
        d = qi - c
        return jnp.where(d == 0, 0, jnp.where(d == 1, 1, 2))

    for g in range(B_KV_HEADS):
        qg = jnp.concatenate(
            [q_ref[:, (g * B_GROUP + r) * B_HEAD_DIM:(g * B_GROUP + r + 1) * B_HEAD_DIM] for r in range(B_GROUP)],
            axis=0)
        qg = (qg * scale).astype(BF16)

        def logits(c2, g=g, qg=qg):
            off = pl.multiple_of(c2 * tw, tw)
            kc = kv_ref[pl.ds(off, tw), g * B_HEAD_DIM:(g + 1) * B_HEAD_DIM].astype(BF16)
            s = lax.dot_general(qg, kc, NT_DIMS, preferred_element_type=F32)
            mb = mb_scr[:, pl.ds(off, tw)]
            bias = jnp.concatenate([bias_ref[g, bias_kind(2 * c2)], bias_ref[g, bias_kind(2 * c2 + 1)]], axis=1)
            return jnp.concatenate([s[r * tq:(r + 1) * tq] + mb for r in range(B_GROUP)], axis=0) + bias

        def max_body(c2, m):
            s = logits(c2)
            s_scr[:, pl.ds(pl.multiple_of(c2 * tw, tw), tw)] = s
            for k in range(nsl):
                m = jnp.maximum(m, s[:, k * 128:(k + 1) * 128])
            return m

        m_scr[...] = lax.fori_loop(0, npair, max_body, jnp.full(m_scr.shape, NEG, F32))
        m_row = _bcast_lanes(jnp.max(m_scr[...], axis=1, keepdims=True), 128)
        m_scr[...] = m_row
        l_scr[...] = jnp.zeros(l_scr.shape, F32)
        acc_scr[...] = jnp.zeros(acc_scr.shape, F32)

        def sum_body(c2, carry, g=g):
            off = pl.multiple_of(c2 * tw, tw)
            vc = kv_ref[pl.ds(off, tw), (B_KV_HEADS + g) * B_HEAD_DIM:(B_KV_HEADS + g + 1) * B_HEAD_DIM].astype(BF16)
            m = m_scr[...]
            p = jnp.exp2(s_scr[:, pl.ds(off, tw)] - jnp.concatenate([m] * nsl, axis=1))
            l = l_scr[...]
            for k in range(nsl):
                l = l + p[:, k * 128:(k + 1) * 128]
            l_scr[...] = l
            acc_scr[...] += jnp.dot(p.astype(BF16), vc, preferred_element_type=F32)
            return carry

        lax.fori_loop(0, npair, sum_body, 0)
        out = acc_scr[...] * (1.0 / jnp.sum(l_scr[...], axis=1, keepdims=True))
        for r in range(B_GROUP):
            h = g * B_GROUP + r
            o_ref[:, h * B_HEAD_DIM:(h + 1) * B_HEAD_DIM] = out[r * tq:(r + 1) * tq]


def _dsa_prompt(z, bias_g, B, T, tq):
    nq = T // tq
    ksel = min(TOPK_MAX, T // 4)
    assert tq >= ksel and tq % 128 == 0 and nq % 2 == 0
    R = B_GROUP * tq
    return pl.pallas_call(
        functools.partial(_dsa_prompt_body, tq=tq, ksel=ksel),
        grid=(B, nq),
        in_specs=[pl.BlockSpec((tq, 1024), lambda b, i: (b * nq + i, Z_QB // 1024)),
                  _resident((T, 512), lambda b, i: (b, Z_KB // 512)),
                  pl.BlockSpec((tq, 512), lambda b, i: (b * nq + i, Z_QI // 512)),
                  _resident((T, 128), lambda b, i: (b, Z_KI // 128)),
                  _resident(bias_g.shape, lambda b, i: (0, 0, 0, 0))],
        out_specs=pl.BlockSpec((tq, 1024), lambda b, i: (b * nq + i, 0)),
        out_shape=jax.ShapeDtypeStruct((B * T, 1024), F32),
        scratch_shapes=[pltpu.VMEM((T, tq), I32), pltpu.VMEM((tq, T), F32),
                        pltpu.VMEM((R, 128), F32), pltpu.VMEM((R, 128), F32), pltpu.VMEM((R, B_HEAD_DIM), F32),
                        pltpu.VMEM((R, T), F32), pltpu.VMEM((T, tq), I16), pltpu.VMEM((T, tq), I16)],
        compiler_params=_cparams(("parallel", "arbitrary")), name="dsa_prompt",
    )(z, z, z, z, bias_g)


PG = 16
QR = 8


def _idx_score(qi32, w32, keys_t_bf):
    s = jnp.dot(qi32, keys_t_bf, preferred_element_type=F32)
    s = jnp.maximum(s * (IDX_DIM ** -0.5), 0.0) * w32
    acc = s[0:QR]
    for h in range(1, IDX_HEADS):
        acc = acc + s[h * QR:(h + 1) * QR]
    return acc * (IDX_HEADS ** -0.5)


def _dsa_s_scores_body(pt_ref, qi_ref, w_ref, *rest, pgs):
    pages, key_ref = rest[:pgs], rest[pgs]
    qi32, w32 = qi_ref[...], w_ref[...]
    for p in range(pgs):
        key_ref[:, p * PAGE_SIZE:(p + 1) * PAGE_SIZE] = _ordered_key(
            _idx_score(qi32, w32, pages[p][...].astype(BF16)))


def _dsa_s_scores(page_table, qi32, w32, kidx_t):
    nb, n_pages = page_table.shape
    pgs = math.gcd(n_pages, 16)
    nj = n_pages // pgs

    def page_spec(p):
        return pl.BlockSpec((None, IDX_DIM, PAGE_SIZE), lambda b, j, pt: (pt[b, j * pgs + p], 0, 0))

    grid_spec = pltpu.PrefetchScalarGridSpec(
        num_scalar_prefetch=1, grid=(nb, nj),
        in_specs=[pl.BlockSpec((None, IDX_HEADS * QR, IDX_DIM), lambda b, j, pt: (b, 0, 0)),
                  pl.BlockSpec((None, IDX_HEADS * QR, 128), lambda b, j, pt: (b, 0, 0))]
                 + [page_spec(p) for p in range(pgs)],
        out_specs=pl.BlockSpec((None, QR, pgs * PAGE_SIZE), lambda b, j, pt: (b, 0, j)))
    return pl.pallas_call(
        functools.partial(_dsa_s_scores_body, pgs=pgs), grid_spec=grid_spec,
        out_shape=jax.ShapeDtypeStruct((nb, QR, n_pages * PAGE_SIZE), I32),
        compiler_params=_cparams(("parallel", "parallel")), name="dsa_sample_scores",
    )(page_table, qi32, w32, *([kidx_t] * pgs))


SQ = 8


def _dsa_s_select_body(key_ref, qi_ref, w_ref, kin_ref, mb_ref, *, n_pages, ksel, t_new):
    R = SQ * QR
    row = lax.broadcasted_iota(I32, (QR, PAGE_SIZE), 0)
    col = lax.broadcasted_iota(I32, (QR, PAGE_SIZE), 1)
    new_ok1 = (col <= row) & (col < t_new)
    key_new = jnp.concatenate(
        [_ordered_key(jnp.where(new_ok1, _idx_score(qi_ref[q], w_ref[q], kin_ref[q]), -jnp.inf)) for q in range(SQ)],
        axis=0)
    new_ok = jnp.concatenate([new_ok1] * SQ, axis=0)

    def slab(s):
        if s == n_pages:
            return key_new
        return key_ref[:, :, s * PAGE_SIZE:(s + 1) * PAGE_SIZE].reshape(R, PAGE_SIZE)

    def count(pred_fn):
        cnt = jnp.zeros((R, 128), F32)
        for s in range(n_pages + 1):
            cnt = cnt + jnp.where(pred_fn(slab(s)), 1.0, 0.0)
        return jnp.sum(cnt, axis=1, keepdims=True)

    def bit_step(it, thr):
        cand = thr + jnp.left_shift(jnp.int32(1), 31 - it)
        tot = count(lambda x: x >= cand)
        return jnp.where(_bcast_lanes(tot, 128) >= ksel, cand, thr)

    thr = lax.fori_loop(0, 32, bit_step, jnp.full((R, 128), INT_MIN, I32))
    need = _bcast_lanes(ksel - count(lambda x: x > thr), 128)
    triu = jnp.where(lax.broadcasted_iota(I32, (128, 128), 0) < lax.broadcasted_iota(I32, (128, 128), 1),
                     1.0, 0.0).astype(BF16)
    ones = jnp.ones((128, 128), BF16)
    carry = jnp.zeros((R, 128), F32)
    for s in range(n_pages + 1):
        x = slab(s)
        eq = x == thr
        eqb = jnp.where(eq, 1.0, 0.0).astype(BF16)
        rank = jnp.dot(eqb, triu, preferred_element_type=F32) + carry
        mb = jnp.where(x > thr, 0.0, jnp.where(eq, jnp.where(rank < need, 0.0, NEG), NEG))
        if s == n_pages:
            mb = jnp.where(new_ok, mb, NEG)
        mb_ref[:, :, s * PAGE_SIZE:(s + 1) * PAGE_SIZE] = mb.reshape(SQ, QR, PAGE_SIZE)
        carry = carry + jnp.dot(eqb, ones, preferred_element_type=F32)


def _dsa_s_select(keys, qi32, w32, ki_new_t, ksel, t_new):
    nb, _, L = keys.shape
    n_pages = L // PAGE_SIZE
    Lp = L + PAGE_SIZE
    assert nb % SQ == 0
    return pl.pallas_call(
        functools.partial(_dsa_s_select_body, n_pages=n_pages, ksel=ksel, t_new=t_new),
        grid=(nb // SQ,),
        in_specs=[pl.BlockSpec((SQ, QR, L), lambda i: (i, 0, 0)),
                  pl.BlockSpec((SQ, IDX_HEADS * QR, IDX_DIM), lambda i: (i, 0, 0)),
                  pl.BlockSpec((SQ, IDX_HEADS * QR, 128), lambda i: (i, 0, 0)),
                  pl.BlockSpec((SQ, IDX_DIM, PAGE_SIZE), lambda i: (i, 0, 0))],
        out_specs=pl.BlockSpec((SQ, QR, Lp), lambda i: (i, 0, 0)),
        out_shape=jax.ShapeDtypeStruct((nb, QR, Lp), F32),
        compiler_params=_cparams(("parallel",)), name="dsa_sample_select",
    )(keys, qi32, w32, ki_new_t)


def _dsa_s_attn_body(pt_ref, q_ref, mb_ref, mbn_ref, kvn_ref, bias_ref, *rest):
    kpages = rest[:PG]
    vpages = rest[PG:2 * PG]
    o_ref = rest[2 * PG]
    m_scr, l_scr, acc_scr = rest[2 * PG + 1:]
    j = pl.program_id(1)
    last = j == pl.num_programs(1) - 1
    scale = (B_HEAD_DIM ** -0.5) * LOG2E

    @pl.when(j == 0)
    def _init():
        m_scr[...] = jnp.full(m_scr.shape, NEG, F32)
        l_scr[...] = jnp.zeros(l_scr.shape, F32)
        acc_scr[...] = jnp.zeros(acc_scr.shape, F32)

    R = B_GROUP * QR
    nrep = B_KV_HEADS * B_GROUP
    qs = [(q_ref[g] * scale).astype(BF16) for g in range(B_KV_HEADS)]

    def near_bias(kind):
        return jnp.concatenate([bias_ref[h, kind] for h in range(B_HEADS)], axis=0)

    def update(s, vs):
        m_old = m_scr[...]
        m_new = jnp.maximum(m_old, jnp.max(s, axis=1, keepdims=True))
        alpha = jnp.exp2(m_old - m_new)
        p = jnp.exp2(s - m_new)
        l_scr[...] = alpha * l_scr[...] + jnp.sum(p, axis=1, keepdims=True)
        pb = p.astype(BF16)
        pv = jnp.concatenate([jnp.dot(pb[g * R:(g + 1) * R], vs[g], preferred_element_type=F32)
                              for g in range(B_KV_HEADS)], axis=0)
        acc_scr[...] = alpha * acc_scr[...] + pv
        m_scr[...] = m_new

    ks = [jnp.concatenate([kpages[p][pl.ds(g, PAGE_SIZE, stride=B_KV_HEADS), :].astype(BF16)
                           for p in range(PG)], axis=0) for g in range(B_KV_HEADS)]
    vs = [jnp.concatenate([vpages[p][pl.ds(g, PAGE_SIZE, stride=B_KV_HEADS), :].astype(BF16)
                           for p in range(PG)], axis=0) for g in range(B_KV_HEADS)]
    s = jnp.concatenate([lax.dot_general(qs[g], ks[g], NT_DIMS, preferred_element_type=F32)
                         for g in range(B_KV_HEADS)], axis=0)
    s = s + jnp.concatenate([mb_ref[...]] * nrep, axis=0)
    nb0 = jnp.where(last, near_bias(0), 0.0)
    s = jnp.concatenate([s[:, :(PG - 1) * PAGE_SIZE], s[:, (PG - 1) * PAGE_SIZE:] + nb0], axis=1)
    update(s, vs)

    @pl.when(last)
    def _fin():
        CW = B_KV_HEADS * B_HEAD_DIM
        kn = [kvn_ref[:, g * B_HEAD_DIM:(g + 1) * B_HEAD_DIM].astype(BF16) for g in range(B_KV_HEADS)]
        vn = [kvn_ref[:, CW + g * B_HEAD_DIM:CW + (g + 1) * B_HEAD_DIM].astype(BF16) for g in range(B_KV_HEADS)]
        sn = jnp.concatenate([lax.dot_general(qs[g], kn[g], NT_DIMS, preferred_element_type=F32)
                              for g in range(B_KV_HEADS)], axis=0)
        sn = sn + jnp.concatenate([mbn_ref[...]] * nrep, axis=0) + near_bias(1)
        update(sn, vn)
        out = acc_scr[...] * (1.0 / l_scr[...])
        for g in range(B_KV_HEADS):
            o_ref[g] = out[g * R:(g + 1) * R]


def _dsa_s_attn(page_table, q32, mb, kv_new, bias_s, ck_rows, cv_rows):
    nb, n_pages = page_table.shape
    assert n_pages % PG == 0
    nj = n_pages // PG
    R = B_GROUP * QR
    PR = PAGE_SIZE * B_KV_HEADS

    def page_spec(p):
        return pl.BlockSpec((PR, B_HEAD_DIM), lambda b, j, pt: (pt[b, j * PG + p], 0))

    grid_spec = pltpu.PrefetchScalarGridSpec(
        num_scalar_prefetch=1, grid=(nb, nj),
        in_specs=[pl.BlockSpec((None, B_KV_HEADS, R, B_HEAD_DIM), lambda b, j, pt: (b, 0, 0, 0)),
                  pl.BlockSpec((None, QR, PG * PAGE_SIZE), lambda b, j, pt: (b, 0, j)),
                  pl.BlockSpec((None, QR, PAGE_SIZE), lambda b, j, pt: (b, 0, n_pages)),
                  pl.BlockSpec((None, PAGE_SIZE, 2 * B_KV_HEADS * B_HEAD_DIM), lambda b, j, pt: (b, 0, 0)),
                  pl.BlockSpec(bias_s.shape, lambda b, j, pt: (0, 0, 0, 0))]
                 + [page_spec(p) for p in range(PG)] + [page_spec(p) for p in range(PG)],
        out_specs=pl.BlockSpec((None, B_KV_HEADS, R, B_HEAD_DIM), lambda b, j, pt: (b, 0, 0, 0)),
        scratch_shapes=[pltpu.VMEM((B_KV_HEADS * R, 1), F32), pltpu.VMEM((B_KV_HEADS * R, 1), F32),
                        pltpu.VMEM((B_KV_HEADS * R, B_HEAD_DIM), F32)])
    return pl.pallas_call(
        _dsa_s_attn_body, grid_spec=grid_spec,
        out_shape=jax.ShapeDtypeStruct((nb, B_KV_HEADS, R, B_HEAD_DIM), F32),
        compiler_params=_cparams(("parallel", "arbitrary")), name="dsa_sample_attn",
    )(page_table, q32, mb, mb, kv_new, bias_s, *([ck_rows] * PG), *([cv_rows] * PG))


def _layer_norm(r, g, b):
    mu = jnp.mean(r, axis=-1, keepdims=True)
    d = r - mu
    var = jnp.mean(d * d, axis=-1, keepdims=True)
    return d * lax.rsqrt(var + LN_EPS) * g + b


def _merge_body(oa_ref, ob_ref, gates_ref, x_ref, g1_ref, sh2_ref, sc2_ref, w_ref, lng_ref, lnb_ref,
                x1_ref, h2_ref, *, alpha):
    D = oa_ref.shape[1]
    merged = _sigmoid(gates_ref[:, 0:D]) * oa_ref[...] + _sigmoid(gates_ref[:, D:2 * D]) * ob_ref[...]
    y = jnp.dot(merged.astype(BF16), w_ref[...], preferred_element_type=F32)
    x1 = _layer_norm(alpha * x_ref[...] + g1_ref[...] * y, lng_ref[...], lnb_ref[...])
    x1_ref[...] = x1
    h2_ref[...] = (x1 * (1.0 + sc2_ref[...]) + sh2_ref[...]).astype(BF16)


def _merge(oa, ob, z, x2d, mod, rows_per_mod, w_out_bf, ln_g, ln_b, alpha, tm):
    N, D = x2d.shape
    row = lambda i: (i, 0)
    const = lambda i: (0, 0)
    return pl.pallas_call(
        functools.partial(_merge_body, alpha=alpha), grid=(N // tm,),
        in_specs=[pl.BlockSpec((tm, D), row), pl.BlockSpec((tm, D), row),
                  pl.BlockSpec((tm, 2 * D), lambda i: (i, Z_GATES // (2 * D))),
                  pl.BlockSpec((tm, D), row),
                  _mod_spec(mod, 2, tm, rows_per_mod, D, 1),
                  _mod_spec(mod, 3, tm, rows_per_mod, D, 1),
                  _mod_spec(mod, 4, tm, rows_per_mod, D, 1),
                  pl.BlockSpec((D, D), const), pl.BlockSpec((1, D), const), pl.BlockSpec((1, D), const)],
        out_specs=[pl.BlockSpec((tm, D), row), pl.BlockSpec((tm, D), row)],
        out_shape=[jax.ShapeDtypeStruct((N, D), F32), jax.ShapeDtypeStruct((N, D), BF16)],
        compiler_params=_cparams(("parallel",)), name="merge_out",
    )(oa, ob, z, x2d, mod, mod, mod, w_out_bf, ln_g, ln_b)


def _top_values(x, n, with_rank=False):
    out = []
    rank = jnp.full(x.shape, float(n), F32) if with_rank else None
    for r in range(n):
        m = jnp.max(x, axis=0, keepdims=True)
        out.append(m)
        if with_rank:
            rank = jnp.where(x == m, float(r), rank)
        if r < n - 1:
            x = jnp.where(x == m, -jnp.inf, x)
    return (out, rank) if with_rank else out


_PEER_PAIRS = [(a, b) for a in range(PEER_TOPK) for b in range(PEER_TOPK) if (a + 1) * (b + 1) <= PEER_TOPK]


def _peer_body(h2_ref, x1_ref, g2_ref, wq_ref, sk_ref, u0_ref, un_ref, vt_ref, lng_ref, lnb_ref, o_ref,
               np_scr, r2_scr, e1_scr, e2_scr, acc_scr, act_a, act_b, w_ab, *, tn, ec, nchunks, alpha):
    e = pl.program_id(1)
    ne = pl.num_programs(1) - 1
    n_i1 = ec // N_KEYS

    @pl.when(e == 0)
    def _select():
        q = jnp.dot(h2_ref[...], wq_ref[...], preferred_element_type=F32)
        for h in range(PEER_HEADS):
            sc = []
            for c in range(2):
                qhc = q[:, (2 * h + c) * N_KEYS:(2 * h + c + 1) * N_KEYS].astype(BF16)
                sc.append(lax.dot_general(sk_ref[c], qhc, NT_DIMS, preferred_element_type=F32))
            s1, s2 = sc
            t1 = _top_values(s1, PEER_TOPK)
            t2, rank2 = _top_values(s2, PEER_TOPK, with_rank=True)
            cand = jnp.concatenate([t1[a] + t2[b] for a, b in _PEER_PAIRS], axis=0)
            thr = _top_values(cand, PEER_TOPK)[-1]
            mx = t1[0] + t2[0]
            zsum = jnp.sum(jnp.where(cand >= thr, jnp.exp(cand - mx), 0.0), axis=0, keepdims=True)
            passed = jnp.where(cand >= thr, 1.0, 0.0)
            npass = jnp.zeros(s1.shape, F32)
            for a in range(PEER_TOPK):
                rows = [i for i, (pa, _) in enumerate(_PEER_PAIRS) if pa == a]
                np_a = passed[rows[0]:rows[0] + 1]
                for i in rows[1:]:
                    np_a = np_a + passed[i:i + 1]
                npass = jnp.where(s1 == t1[a], np_a, npass)
            np_scr[h] = npass
            r2_scr[h] = rank2.astype(BF16)
            e1_scr[h] = jnp.exp(s1 - t1[0]) * (1.0 / zsum)
            e2_scr[h] = jnp.exp(s2 - t2[0]).astype(BF16)
        act_a[...] = lax.dot_general(u0_ref[...], h2_ref[...], NT_DIMS, preferred_element_type=F32)
        w_ab[1] = jnp.zeros(w_ab.shape[1:], BF16)
        acc_scr[...] = jnp.zeros(acc_scr.shape, F32)

    PK = 16

    def chunk(act_cur, act_nxt, w_cur, w_prev):
        acc_scr[...] += jnp.dot(vt_ref[...], w_prev[...], preferred_element_type=F32)

        act_nxt[...] = lax.dot_general(un_ref[...], h2_ref[...], NT_DIMS, preferred_element_type=F32)
        for jj in range(n_i1):
            i1 = e * n_i1 + jj
            a = act_cur[jj * N_KEYS:(jj + 1) * N_KEYS, :]
            gsum = jnp.zeros((N_KEYS // PK, PK, tn), BF16)
            for h in range(PEER_HEADS):
                npass_row = jnp.broadcast_to(np_scr[h, pl.ds(i1, 1), :], (PK, tn)).astype(BF16)
                e1_row = jnp.broadcast_to(e1_scr[h, pl.ds(i1, 1), :], (PK, tn)).astype(BF16)
                sel = r2_scr[h].reshape(N_KEYS // PK, PK, tn) < npass_row
                e2 = e2_scr[h].reshape(N_KEYS // PK, PK, tn)
                gsum = gsum + jnp.where(sel, e2, jnp.zeros_like(e2)) * e1_row
            gelu = 0.5 * a * (1.0 + lax.erf(a * (2.0 ** -0.5)))
            w_cur[jj * N_KEYS:(jj + 1) * N_KEYS, :] = gsum.reshape(N_KEYS, tn) * gelu.astype(BF16)

    @pl.when((e < ne) & (e % 2 == 0))
    def _even():
        chunk(act_a, act_b, w_ab.at[e % 2], w_ab.at[1 - e % 2])

    @pl.when((e < ne) & (e % 2 == 1))
    def _odd():
        chunk(act_b, act_a, w_ab.at[e % 2], w_ab.at[1 - e % 2])

    @pl.when(e == ne)
    def _fin():
        acc = acc_scr[...] + jnp.dot(vt_ref[...], w_ab[(nchunks - 1) % 2], preferred_element_type=F32)
        y = acc.T
        o_ref[...] = _layer_norm(alpha * x1_ref[...] + g2_ref[...] * y, lng_ref[...], lnb_ref[...])


def _peer(h2, x1, mod, rows_per_mod, wq_bf, sk_bf, u_bf, vt_bf, ln_g, ln_b, alpha, tn, ec):
    N, D = x1.shape
    E = u_bf.shape[0]
    ne = E // ec
    row = lambda i, e: (i, 0)
    const2 = lambda i, e: (0, 0)
    if mod.ndim == 3:
        g2_spec = pl.BlockSpec((None, 1, D), lambda i, e: ((i * tn) // rows_per_mod, 0, 5))
    else:
        g2_spec = pl.BlockSpec((tn, D), lambda i, e: (i, 5))
    return pl.pallas_call(
        functools.partial(_peer_body, tn=tn, ec=ec, nchunks=ne, alpha=alpha), grid=(N // tn, ne + 1),
        in_specs=[pl.BlockSpec((tn, D), row), pl.BlockSpec((tn, D), row), g2_spec,
                  _resident(wq_bf.shape, const2),
                  _resident(sk_bf.shape, lambda i, e: (0, 0, 0)),
                  _resident((ec, D), const2),
                  pl.BlockSpec((ec, D), lambda i, e: (jnp.minimum(e + 1, ne - 1), 0)),
                  pl.BlockSpec((D, ec), lambda i, e: (0, jnp.maximum(e - 1, 0))),
                  pl.BlockSpec((1, D), const2), pl.BlockSpec((1, D), const2)],
        out_specs=pl.BlockSpec((tn, D), row),
        out_shape=jax.ShapeDtypeStruct((N, D), F32),
        scratch_shapes=[pltpu.VMEM((PEER_HEADS, N_KEYS, tn), F32), pltpu.VMEM((PEER_HEADS, N_KEYS, tn), BF16),
                        pltpu.VMEM((PEER_HEADS, N_KEYS, tn), F32), pltpu.VMEM((PEER_HEADS, N_KEYS, tn), BF16),
                        pltpu.VMEM((D, tn), F32),
                        pltpu.VMEM((ec, tn), F32), pltpu.VMEM((ec, tn), F32), pltpu.VMEM((2, ec, tn), BF16)],
        compiler_params=_cparams(("parallel", "arbitrary")), name="peer",
    )(h2, x1, mod, wq_bf, sk_bf, u_bf, u_bf, vt_bf, ln_g, ln_b)


def _pack_w_in(w_in):
    D = w_in.shape[0]
    idx_w = w_in[:, 5632:5956]
    pad = jnp.zeros((D, Z_GATES - Z_QI - idx_w.shape[1]), w_in.dtype)
    return jnp.concatenate([w_in[:, :5632], idx_w, pad, w_in[:, 5956:]], axis=1).astype(BF16)


def _prompt_buckets(tq):
    i = np.arange(tq)[:, None]
    j = np.arange(tq)[None, :]
    return np.stack([_t5_bucket_np(np.maximum(i - j, 0)), _t5_bucket_np(tq + i - j)])


def _sample_buckets():
    t = np.arange(QR)[:, None]
    o = np.arange(PAGE_SIZE)[None, :]
    return np.stack([_t5_bucket_np(PAGE_SIZE + t - o), _t5_bucket_np(np.maximum(t - o, 0))])


def _layer_tail(oa, ob, z, x2d, mod, rows_per_mod, wts, alpha, tm, tn_peer):
    x1, h2 = _merge(oa, ob, z, x2d, mod, rows_per_mod, wts["w_out"], wts["ln1_g"], wts["ln1_b"], alpha, tm)
    return _peer(h2, x1, mod, rows_per_mod, wts["w_peer_q"], wts["sub_keys"], wts["peer_u"], wts["peer_vt"],
                 wts["ln2_g"], wts["ln2_b"], alpha, tn_peer, 1024)


def kernel(x_prompt, x_sample, cache_k, cache_v, cache_kidx, state_hgrn, page_table, c_prompt, c_sample,
           w_ada, b_ada, w_in, lb_param, a_gnorm, t5_table, w_out, ln1_g, ln1_b, w_peer_q, peer_sub_keys,
           peer_u, peer_v, ln2_g, ln2_b):
    depth = w_ada.shape[0]
    assert depth == 1 and lb_param.shape[0] == 2
    alpha = (2.0 * depth) ** 0.25
    B, T, D = x_prompt.shape
    NB, TS, _ = x_sample.shape
    n_pages = page_table.shape[1]
    past_len = n_pages * PAGE_SIZE

    wts = dict(
        w_out=w_out[0].astype(BF16), ln1_g=ln1_g, ln1_b=ln1_b, ln2_g=ln2_g, ln2_b=ln2_b,
        w_peer_q=w_peer_q[0].astype(BF16), sub_keys=peer_sub_keys[0].astype(BF16),
        peer_u=peer_u[0].astype(BF16), peer_vt=peer_v[0].T.astype(BF16))
    w_in_bf = _pack_w_in(w_in[0])
    gnorm = a_gnorm.reshape(1, A_DK)

    mod = _ada(jnp.concatenate([c_prompt, c_sample], axis=0), w_ada[0].astype(BF16), b_ada)
    mod_p = mod[:B].reshape(B, 1, 6 * D)
    mod_s = jnp.repeat(mod[B:], TS, axis=0)

    xp = x_prompt.reshape(B * T, D)
    zp, kp_rows, vp_rows = _inproj(xp, mod_p, T, w_in_bf, 256, 1024)
    oa_p, s_p = _hgrn(zp, jnp.zeros((B, A_HEADS, A_DK, A_DK), F32), lb_param, gnorm, B, T, 256, A_CHUNK, 256, 8)
    tq = 256
    bias_p = _bias_tiles(t5_table, _prompt_buckets(tq))
    bias_g = bias_p.reshape(B_KV_HEADS, B_GROUP, 2, tq, tq).transpose(0, 2, 1, 3, 4)
    bias_g = bias_g.reshape(B_KV_HEADS, 2, B_GROUP * tq, tq)
    bias_g = jnp.concatenate([bias_g, jnp.zeros((B_KV_HEADS, 1, B_GROUP * tq, tq), F32)], axis=1)
    ob_p = _dsa_prompt(zp, bias_g, B, T, tq)
    y_p = _layer_tail(oa_p, ob_p, zp, xp, mod_p, T, wts, alpha, 256, 512)

    xs = x_sample.reshape(NB * TS, D)
    zs, ks_rows, vs_rows = _inproj(xs, mod_s, 1, w_in_bf, min(256, NB * TS), 1024)
    zs3 = zs.reshape(NB, TS, Z_WIDTH)
    zs_pad = jnp.pad(zs3, ((0, 0), (0, QR - TS), (0, 0))).reshape(NB * QR, Z_WIDTH)
    assert TS <= QR
    oa_s8, s_s = _hgrn(zs_pad, state_hgrn[0], lb_param, gnorm, NB, QR, QR, QR, TS, A_HEADS)
    oa_s = oa_s8.reshape(NB, QR, D)[:, :TS].reshape(NB * TS, D)

    qi_s = zs3[:, :, Z_QI:Z_QI + IDX_HEADS * IDX_DIM].reshape(NB, TS, IDX_HEADS, IDX_DIM)
    qi32 = jnp.pad(qi_s.transpose(0, 2, 1, 3), ((0, 0), (0, 0), (0, QR - TS), (0, 0)))
    qi32 = qi32.reshape(NB, IDX_HEADS * QR, IDX_DIM).astype(BF16)
    wi_s = zs3[:, :, Z_WI:Z_WI + IDX_HEADS].transpose(0, 2, 1)
    w32 = jnp.pad(wi_s, ((0, 0), (0, 0), (0, QR - TS))).reshape(NB, IDX_HEADS * QR, 1)
    w32 = jnp.broadcast_to(w32, (NB, IDX_HEADS * QR, 128))
    ki_new = jnp.pad(zs3[:, :, Z_KI:Z_KI + IDX_DIM], ((0, 0), (0, PAGE_SIZE - TS), (0, 0))).astype(BF16)
    ki_new_t = jnp.swapaxes(ki_new, 1, 2)
    kv_new = jnp.pad(zs3[:, :, Z_KB:Z_KB + 512], ((0, 0), (0, PAGE_SIZE - TS), (0, 0)))
    q_s = zs3[:, :, Z_QB:Z_QB + 1024].reshape(NB, TS, B_KV_HEADS, B_GROUP, B_HEAD_DIM)
    q32 = jnp.pad(q_s.transpose(0, 2, 3, 1, 4), ((0, 0), (0, 0), (0, 0), (0, QR - TS), (0, 0)))
    q32 = q32.reshape(NB, B_KV_HEADS, B_GROUP * QR, B_HEAD_DIM)

    ksel = min(TOPK_MAX, (past_len + TS) // 4)
    kidx_t = jnp.swapaxes(cache_kidx[0], 1, 2)
    keys_s = _dsa_s_scores(page_table, qi32, w32, kidx_t)
    mb_s = _dsa_s_select(keys_s, qi32, w32, ki_new_t, ksel, TS)
    bias_s = _bias_tiles(t5_table, _sample_buckets())
    n_pool = cache_k.shape[1]
    ck = cache_k.reshape(n_pool * PAGE_SIZE * B_KV_HEADS, B_HEAD_DIM)
    cv = cache_v.reshape(n_pool * PAGE_SIZE * B_KV_HEADS, B_HEAD_DIM)
    o32 = _dsa_s_attn(page_table, q32, mb_s, kv_new, bias_s, ck, cv)
    ob_s = o32.reshape(NB, B_KV_HEADS, B_GROUP, QR, B_HEAD_DIM)[:, :, :, :TS]
    ob_s = ob_s.transpose(0, 3, 1, 2, 4).reshape(NB * TS, D)
    y_s = _layer_tail(oa_s, ob_s, zs, xs, mod_s, 1, wts, alpha, NB * TS, NB * TS)

    def kv_out(z, k_rows, v_rows, nb, t):
        k = k_rows.reshape(1, nb, t, B_KV_HEADS, B_HEAD_DIM)
        v = v_rows.reshape(1, nb, t, B_KV_HEADS, B_HEAD_DIM)
        ki = z[:, Z_KI:Z_KI + IDX_DIM].reshape(1, nb, t, IDX_DIM)
        return k, v, ki

    kp, vp, ip = kv_out(zp, kp_rows, vp_rows, B, T)
    ks, vs, is_ = kv_out(zs, ks_rows, vs_rows, NB, TS)
    return (y_p.reshape(B, T, D), y_s.reshape(NB, TS, D), kp, vp, ip, s_p[None],
            ks, vs, is_, s_s[None])
```

```python
import functools
import math

import numpy as np
import jax
import jax.numpy as jnp
from jax import lax
from jax.experimental import pallas as pl
from jax.experimental.pallas import tpu as pltpu

F32 = jnp.float32
BF16 = jnp.bfloat16
I32 = jnp.int32
I16 = jnp.int16

A_HEADS = 8
A_DK = 128
A_CHUNK = 32
B_HEADS = 8
B_KV_HEADS = 2
B_GROUP = 4
B_HEAD_DIM = 128
IDX_HEADS = 4
IDX_DIM = 64
TOPK_MAX = 256
N_BUCKETS = 32
MAX_DISTANCE = 128
PEER_HEADS = 8
N_KEYS = 128
PEER_TOPK = 16
PAGE_SIZE = 128
LN_EPS = 1e-5
RMS_EPS = 1e-6

Z_A = 0
Z_QB = 4096
Z_KB = 5120
Z_VB = 5376
Z_QI = 5632
Z_KI = 5888
Z_WI = 5952
Z_GATES = 6144
Z_WIDTH = 8192

NEG = -1e30
INT_MIN = -2 ** 31
I16_MIN = -2 ** 15
LOG2E = 1.4426950408889634
NT_DIMS = (((1,), (1,)), ((), ()))
VMEM_LIMIT = 56 * 1024 * 1024


def _cparams(sem):
    return pltpu.CompilerParams(dimension_semantics=sem, vmem_limit_bytes=VMEM_LIMIT)


def _resident(shape, index_map):
    return pl.BlockSpec(shape, index_map, pipeline_mode=pl.Buffered(1))


def _sigmoid(x):
    return jax.nn.sigmoid(x)


def _ada_body(c_ref, w_ref, b_ref, o_ref):
    c = c_ref[...]
    s = c * _sigmoid(c)
    o_ref[...] = jnp.dot(s.astype(BF16), w_ref[...], preferred_element_type=F32) + b_ref[...]


def _ada(c, w_bf, b):
    R, D = c.shape
    N = w_bf.shape[1]
    tn = N // 4
    return pl.pallas_call(
        _ada_body, grid=(N // tn,),
        in_specs=[pl.BlockSpec((R, D), lambda j: (0, 0)),
                  pl.BlockSpec((D, tn), lambda j: (0, j)),
                  pl.BlockSpec((1, tn), lambda j: (0, j))],
        out_specs=pl.BlockSpec((R, tn), lambda j: (0, j)),
        out_shape=jax.ShapeDtypeStruct((R, N), F32),
        compiler_params=_cparams(("parallel",)), name="ada")(c, w_bf, b)


def _mod_spec(mod, blk, tm, rows_per_mod, D, nidx):
    if mod.ndim == 3:
        if nidx == 1:
            return pl.BlockSpec((None, 1, D), lambda i: ((i * tm) // rows_per_mod, 0, blk))
        return pl.BlockSpec((None, 1, D), lambda i, j: ((i * tm) // rows_per_mod, 0, blk))
    if nidx == 1:
        return pl.BlockSpec((tm, D), lambda i: (i, blk))
    return pl.BlockSpec((tm, D), lambda i, j: (i, blk))


def _inproj_body(x_ref, sh_ref, sc_ref, w_ref, o_ref, k_ref, v_ref, *, tn):
    tm = x_ref.shape[0]
    h = (x_ref[...] * (1.0 + sc_ref[...]) + sh_ref[...]).astype(BF16)
    for j in range(o_ref.shape[1] // tn):
        zt = jnp.dot(h, w_ref[:, j * tn:(j + 1) * tn], preferred_element_type=F32)
        o_ref[:, j * tn:(j + 1) * tn] = zt
        if j * tn <= Z_KB and Z_VB + B_KV_HEADS * B_HEAD_DIM <= (j + 1) * tn:
            for g in range(B_KV_HEADS):
                ko = Z_KB - j * tn + g * B_HEAD_DIM
                vo = Z_VB - j * tn + g * B_HEAD_DIM
                k_ref[pl.ds(g, tm, stride=B_KV_HEADS), :] = zt[:, ko:ko + B_HEAD_DIM]
                v_ref[pl.ds(g, tm, stride=B_KV_HEADS), :] = zt[:, vo:vo + B_HEAD_DIM]


def _inproj(x2d, mod, rows_per_mod, w_bf, tm, tn):
    N, D = x2d.shape
    W = w_bf.shape[1]
    kv_spec = pl.BlockSpec((tm * B_KV_HEADS, B_HEAD_DIM), lambda i: (i, 0))
    kv_shape = jax.ShapeDtypeStruct((N * B_KV_HEADS, B_HEAD_DIM), F32)
    return pl.pallas_call(
        functools.partial(_inproj_body, tn=tn), grid=(N // tm,),
        in_specs=[pl.BlockSpec((tm, D), lambda i: (i, 0)),
                  _mod_spec(mod, 0, tm, rows_per_mod, D, 1),
                  _mod_spec(mod, 1, tm, rows_per_mod, D, 1),
                  _resident((D, W), lambda i: (0, 0))],
        out_specs=[pl.BlockSpec((tm, W), lambda i: (i, 0)), kv_spec, kv_spec],
        out_shape=[jax.ShapeDtypeStruct((N, W), F32), kv_shape, kv_shape],
        compiler_params=_cparams(("parallel",)), name="inproj")(x2d, mod, mod, w_bf)


def _hgrn_body(q_ref, f_ref, i_ref, g_ref, lbp_ref, gn_ref, s0_ref, o_ref, sout_ref, st_scr,
               *, C, Tb, t_valid, HB):
    tb = pl.program_id(2)
    shift = int(math.log2(C))
    nchunk = Tb // C
    r = lax.broadcasted_iota(I32, (Tb, Tb), 0)
    c = lax.broadcasted_iota(I32, (Tb, Tb), 1)
    tril = (jnp.right_shift(r, shift) == jnp.right_shift(c, shift)) & (c <= r)
    trilf = jnp.where(tril, 1.0, 0.0)
    Tp = max(Tb, 128)
    colchunk = jnp.right_shift(lax.broadcasted_iota(I32, (A_DK, Tp), 1), shift)

    lbp = lbp_ref[...]
    mx = jnp.max(lbp, axis=0, keepdims=True)
    e = jnp.exp(lbp - mx)
    lb_all = e[0:1, :] / jnp.sum(e, axis=0, keepdims=True)

    @pl.when(tb == 0)
    def _init():
        for hh in range(HB):
            st_scr[hh] = s0_ref[hh].T

    W = HB * A_DK
    f = lb_all + (1.0 - lb_all) * _sigmoid(f_ref[...])
    lf = jnp.log(f)
    k = 1.0 - f
    if t_valid < Tb:
        ok = lax.broadcasted_iota(I32, (Tb, W), 0) < t_valid
        lf = jnp.where(ok, lf, 0.0)
        k = jnp.where(ok, k, 0.0)
    bcum = jnp.dot(trilf, lf, precision=lax.Precision.HIGHEST,
                   preferred_element_type=F32)
    tot = jnp.concatenate(
        [jnp.broadcast_to(bcum[(j + 1) * C - 1:(j + 1) * C, :], (C, W)) for j in range(nchunk)], axis=0)
    qa = q_ref[...]
    qe_b = ((qa * _sigmoid(qa)) * jnp.exp(bcum)).astype(BF16)
    ke_b = (k * jnp.exp(-bcum)).astype(BF16)
    kd = k * jnp.exp(tot - bcum)
    v = i_ref[...]
    v_b = v.astype(BF16)
    if Tb < 128:
        zpad = jnp.zeros((128 - Tb, W), F32)
        v_p = jnp.concatenate([v, zpad], axis=0)
        kd_p = jnp.concatenate([kd, zpad], axis=0)
    else:
        v_p, kd_p = v, kd
    kd_b = kd_p.astype(BF16)
    heads = [(hh * A_DK, (hh + 1) * A_DK) for hh in range(HB)]

    scores = [jnp.where(tril, lax.dot_general(qe_b[:, lo:hi], ke_b[:, lo:hi], NT_DIMS,
                                              preferred_element_type=F32), 0.0).astype(BF16) for lo, hi in heads]
    o = [jnp.dot(scores[hh], v_b[:, lo:hi], preferred_element_type=F32) for hh, (lo, hi) in enumerate(heads)]
    vT = [v_p[:, lo:hi].T for lo, hi in heads]
    ST = [st_scr[hh] for hh in range(HB)]
    o_inter = [[] for _ in range(HB)]
    for j in range(nchunk):
        for hh, (lo, hi) in enumerate(heads):
            o_inter[hh].append(lax.dot_general(qe_b[j * C:(j + 1) * C, lo:hi], ST[hh].astype(BF16), NT_DIMS,
                                               preferred_element_type=F32))
            d_c = jnp.exp(tot[j * C:j * C + 1, lo:hi])
            vTm = jnp.where(colchunk == j, vT[hh], 0.0) if nchunk > 1 else vT[hh]
            ST[hh] = ST[hh] * d_c + jnp.dot(vTm.astype(BF16), kd_b[:, lo:hi], preferred_element_type=F32)
    gn = gn_ref[...]
    ga = g_ref[...]
    gate = ga * _sigmoid(ga)
    for hh, (lo, hi) in enumerate(heads):
        st_scr[hh] = ST[hh]
        oh = o[hh] + jnp.concatenate(o_inter[hh], axis=0)
        ms = jnp.mean(oh * oh, axis=-1, keepdims=True)
        o_ref[:, lo:hi] = oh * lax.rsqrt(ms + RMS_EPS) * gn * gate[:, lo:hi]

    @pl.when(tb == pl.num_programs(2) - 1)
    def _fin():
        for hh in range(HB):
            sout_ref[hh] = st_scr[hh].T


def _hgrn(z, s0, lb_param, gnorm, B, T, Tb, C, t_valid, HB):
    nt = T // Tb
    H = A_HEADS
    W = HB * A_DK
    ng = H // HB

    def col(group):
        return pl.BlockSpec((Tb, W), lambda b, h, t: (b * nt + t, group * ng + h))

    return pl.pallas_call(
        functools.partial(_hgrn_body, C=C, Tb=Tb, t_valid=t_valid, HB=HB),
        grid=(B, ng, nt),
        in_specs=[col(0), col(1), col(2), col(3),
                  pl.BlockSpec((2, W), lambda b, h, t: (0, h)),
                  pl.BlockSpec((1, A_DK), lambda b, h, t: (0, 0)),
                  pl.BlockSpec((None, HB, A_DK, A_DK), lambda b, h, t: (b, h, 0, 0))],
        out_specs=[pl.BlockSpec((Tb, W), lambda b, h, t: (b * nt + t, h)),
                   pl.BlockSpec((None, HB, A_DK, A_DK), lambda b, h, t: (b, h, 0, 0))],
        out_shape=[jax.ShapeDtypeStruct((B * T, H * A_DK), F32),
                   jax.ShapeDtypeStruct((B, H, A_DK, A_DK), F32)],
        scratch_shapes=[pltpu.VMEM((HB, A_DK, A_DK), F32)],
        compiler_params=_cparams(("parallel", "parallel", "arbitrary")), name="hgrn",
    )(z, z, z, z, lb_param, gnorm, s0)


def _t5_bucket_np(rel):
    rel = np.asarray(rel, np.int64)
    max_exact = N_BUCKETS // 2
    relf = np.maximum(rel, 1).astype(np.float64)
    large = max_exact + (np.log(relf / max_exact) / math.log(MAX_DISTANCE / max_exact)
                         * (N_BUCKETS - max_exact)).astype(np.int64)
    large = np.minimum(large, N_BUCKETS - 1)
    return np.where(rel < max_exact, rel, large).astype(np.int32)


def _bias_body(tab_ref, bkt_ref, o_ref):
    h = pl.program_id(0)
    b = bkt_ref[...]
    acc = jnp.zeros(b.shape, F32)
    for i in range(N_BUCKETS):
        acc = jnp.where(b == i, tab_ref[i, h], acc)
    o_ref[...] = (acc - tab_ref[N_BUCKETS - 1, h]) * LOG2E


def _bias_tiles(t5_table, buckets):
    K, R, Cc = buckets.shape
    return pl.pallas_call(
        _bias_body, grid=(B_HEADS, K),
        in_specs=[pl.BlockSpec(memory_space=pltpu.SMEM),
                  pl.BlockSpec((None, R, Cc), lambda h, k: (k, 0, 0))],
        out_specs=pl.BlockSpec((None, None, R, Cc), lambda h, k: (h, k, 0, 0)),
        out_shape=jax.ShapeDtypeStruct((B_HEADS, K, R, Cc), F32),
        compiler_params=_cparams(("parallel", "parallel")), name="t5_bias")(t5_table, jnp.asarray(buckets))


def _ordered_key(score):
    score = jnp.where(score == 0.0, 0.0, score)
    bits = pltpu.bitcast(score, I32)
    return jnp.where(bits < 0, jnp.bitwise_xor(bits, jnp.int32(0x7FFFFFFF)), bits)


def _bcast_lanes(col, n):
    return jnp.broadcast_to(col, (col.shape[0], n))


def _dsa_prompt_body(q_ref, kv_ref, idxq_ref, ki_ref, bias_ref, o_ref,
                     key_scr, mb_scr, m_scr, l_scr, acc_scr, s_scr, hi_scr, lo_scr, *, tq, ksel):
    qi = pl.program_id(1)
    nvalid = qi + 1
    npair = (nvalid + 1) // 2
    tw = 2 * tq

    idxq = idxq_ref[...]
    assert (IDX_DIM ** -0.5) == 0.125 and (IDX_HEADS ** -0.5) == 0.5
    q_idx = [(idxq[:, h * IDX_DIM:(h + 1) * IDX_DIM] * (IDX_DIM ** -0.5)).astype(BF16) for h in range(IDX_HEADS)]
    w_t = idxq[:, Z_KI - Z_QI:Z_KI - Z_QI + 128].T
    w_rows = [w_t[Z_WI - Z_KI + h:Z_WI - Z_KI + h + 1, :] * (IDX_HEADS ** -0.5) for h in range(IDX_HEADS)]
    key_i = lax.broadcasted_iota(I32, (tq, tq), 0)
    qry_t = qi * tq + lax.broadcasted_iota(I32, (tq, tq), 1)

    def idx_pair(c2, carry):
        for k in range(2):
            off = pl.multiple_of((2 * c2 + k) * tq, tq)
            kic = ki_ref[pl.ds(off, tq), 0:IDX_DIM].astype(BF16)
            acc = jnp.zeros((tq, tq), F32)
            for h in range(IDX_HEADS):
                s = lax.dot_general(kic, q_idx[h], NT_DIMS, preferred_element_type=F32)
                acc = acc + jnp.maximum(s, 0.0) * w_rows[h]
            score = jnp.where(off + key_i <= qry_t, acc, -jnp.inf)
            key = _ordered_key(score)
            key_scr[pl.ds(off, tq), :] = key
            hi_scr[pl.ds(off, tq), :] = jnp.right_shift(key, 16).astype(I16)
            lo_scr[pl.ds(off, tq), :] = (jnp.bitwise_and(key, 0xFFFF) - 32768).astype(I16)
        return carry

    lax.fori_loop(0, npair, idx_pair, 0)

    NACC = 8
    PK = 16

    def tile16(v32):
        return jnp.broadcast_to(v32, (PK, tq)).astype(I16)

    def count16(ref, pred_fn):
        def body(c, cnt):
            off = pl.multiple_of(c * tw, tw)
            x = ref[pl.ds(off, tw), :].reshape(tw // (PK * NACC), NACC, PK, tq)
            one = jnp.ones(x.shape, I16)
            hit = jnp.where(pred_fn(x), one, jnp.zeros_like(one))
            for gi in range(hit.shape[0]):
                cnt = cnt + hit[gi]
            return cnt
        cnt = lax.fori_loop(0, npair, body, jnp.zeros((NACC, PK, tq), I16))
        return jnp.sum(jnp.sum(cnt.astype(F32), axis=0), axis=0, keepdims=True)

    def search16(ref, target):
        def bit_step(it, v):
            cand = v + jnp.left_shift(jnp.int32(1), 15 - it)
            c16 = tile16(cand)
            return jnp.where(count16(ref, lambda x: x >= c16) >= target, cand, v)
        return lax.fori_loop(0, 16, bit_step, jnp.full((1, tq), I16_MIN, I32))

    hi_k = search16(hi_scr, ksel)
    hi_t = tile16(hi_k)
    n_above = count16(hi_scr, lambda x: x > hi_t)

    def keep_lo(c, carry):
        off = pl.multiple_of(c * tw, tw)
        h = hi_scr[pl.ds(off, tw), :].reshape(tw // PK, PK, tq)
        l = lo_scr[pl.ds(off, tw), :].reshape(tw // PK, PK, tq)
        lo_scr[pl.ds(off, tw), :] = jnp.where(h == hi_t, l, jnp.full(l.shape, I16_MIN, I16)).reshape(tw, tq)
        return carry

    lax.fori_loop(0, npair, keep_lo, 0)
    lo_k = search16(lo_scr, ksel - n_above)
    lo_t = tile16(lo_k)
    thr = hi_k * 65536 + (lo_k + 32768)
    need = ksel - n_above - count16(lo_scr, lambda x: x > lo_t)
    tril = jnp.where(lax.broadcasted_iota(I32, (tq, tq), 1) < key_i, 1.0, 0.0).astype(BF16)

    def mask_pair(c2, carry):
        for k in range(2):
            off = pl.multiple_of((2 * c2 + k) * tq, tq)
            x = key_scr[pl.ds(off, tq), :]
            eq = x == thr
            eqf = jnp.where(eq, 1.0, 0.0)
            rank = jnp.dot(tril, eqf.astype(BF16), preferred_element_type=F32) + carry
            mb = jnp.where(x > thr, 0.0, jnp.where(eq, jnp.where(rank < need, 0.0, NEG), NEG))
            mb = jnp.where(off + key_i <= qry_t, mb, NEG)
            mb_scr[:, pl.ds(off, tq)] = mb.T
            carry = carry + jnp.sum(eqf, axis=0, keepdims=True)
        return carry

    lax.fori_loop(0, npair, mask_pair, jnp.zeros((1, tq), F32))

    scale = (B_HEAD_DIM ** -0.5) * LOG2E
    nsl = tw // 128

    def bias_kind(c):
        d = qi - c
        return jnp.where(d == 0, 0, jnp.where(d == 1, 1, 2))

    for g in range(B_KV_HEADS):
        qg = jnp.concatenate(
            [q_ref[:, (g * B_GROUP + r) * B_HEAD_DIM:(g * B_GROUP + r + 1) * B_HEAD_DIM] for r in range(B_GROUP)],
            axis=0)
        qg = (qg * scale).astype(BF16)

        def logits(c2, g=g, qg=qg):
            off = pl.multiple_of(c2 * tw, tw)
            kc = kv_ref[pl.ds(off, tw), g * B_HEAD_DIM:(g + 1) * B_HEAD_DIM].astype(BF16)
            s = lax.dot_general(qg, kc, NT_DIMS, preferred_element_type=F32)
            mb = mb_scr[:, pl.ds(off, tw)]
            bias = jnp.concatenate([bias_ref[g, bias_kind(2 * c2)], bias_ref[g, bias_kind(2 * c2 + 1)]], axis=1)
            return jnp.concatenate([s[r * tq:(r + 1) * tq] + mb for r in range(B_GROUP)], axis=0) + bias

        def max_body(c2, m):
            s = logits(c2)
            s_scr[:, pl.ds(pl.multiple_of(c2 * tw, tw), tw)] = s
            for k in range(nsl):
                m = jnp.maximum(m, s[:, k * 128:(k + 1) * 128])
            return m

        m_scr[...] = lax.fori_loop(0, npair, max_body, jnp.full(m_scr.shape, NEG, F32))
        m_row = _bcast_lanes(jnp.max(m_scr[...], axis=1, keepdims=True), 128)
        m_scr[...] = m_row
        l_scr[...] = jnp.zeros(l_scr.shape, F32)
        acc_scr[...] = jnp.zeros(acc_scr.shape, F32)

        def sum_body(c2, carry, g=g):
            off = pl.multiple_of(c2 * tw, tw)
            vc = kv_ref[pl.ds(off, tw), (B_KV_HEADS + g) * B_HEAD_DIM:(B_KV_HEADS + g + 1) * B_HEAD_DIM].astype(BF16)
            m = m_scr[...]
            p = jnp.exp2(s_scr[:, pl.ds(off, tw)] - jnp.concatenate([m] * nsl, axis=1))
            l = l_scr[...]
            for k in range(nsl):
                l = l + p[:, k * 128:(k + 1) * 128]
            l_scr[...] = l
            acc_scr[...] += jnp.dot(p.astype(BF16), vc, preferred_element_type=F32)
            return carry

        lax.fori_loop(0, npair, sum_body, 0)
        out = acc_scr[...] * (1.0 / jnp.sum(l_scr[...], axis=1, keepdims=True))
        for r in range(B_GROUP):
            h = g * B_GROUP + r
            o_ref[:, h * B_HEAD_DIM:(h + 1) * B_HEAD_DIM] = out[r * tq:(r + 1) * tq]


def _dsa_prompt(z, bias_g, B, T, tq):
    nq = T // tq
    ksel = min(TOPK_MAX, T // 4)
    assert tq >= ksel and tq % 128 == 0 and nq % 2 == 0
    R = B_GROUP * tq
    return pl.pallas_call(
        functools.partial(_dsa_prompt_body, tq=tq, ksel=ksel),
        grid=(B, nq),
        in_specs=[pl.BlockSpec((tq, 1024), lambda b, i: (b * nq + i, Z_QB // 1024)),
                  _resident((T, 512), lambda b, i: (b, Z_KB // 512)),
                  pl.BlockSpec((tq, 512), lambda b, i: (b * nq + i, Z_QI // 512)),
                  _resident((T, 128), lambda b, i: (b, Z_KI // 128)),
                  _resident(bias_g.shape, lambda b, i: (0, 0, 0, 0))],
        out_specs=pl.BlockSpec((tq, 1024), lambda b, i: (b * nq + i, 0)),
        out_shape=jax.ShapeDtypeStruct((B * T, 1024), F32),
        scratch_shapes=[pltpu.VMEM((T, tq), I32), pltpu.VMEM((tq, T), F32),
                        pltpu.VMEM((R, 128), F32), pltpu.VMEM((R, 128), F32), pltpu.VMEM((R, B_HEAD_DIM), F32),
                        pltpu.VMEM((R, T), F32), pltpu.VMEM((T, tq), I16), pltpu.VMEM((T, tq), I16)],
        compiler_params=_cparams(("parallel", "arbitrary")), name="dsa_prompt",
    )(z, z, z, z, bias_g)


PG = 16
QR = 8


def _idx_score(qi32, w32, keys_t_bf):
    s = jnp.dot(qi32, keys_t_bf, preferred_element_type=F32)
    s = jnp.maximum(s * (IDX_DIM ** -0.5), 0.0) * w32
    acc = s[0:QR]
    for h in range(1, IDX_HEADS):
        acc = acc + s[h * QR:(h + 1) * QR]
    return acc * (IDX_HEADS ** -0.5)


def _dsa_s_scores_body(pt_ref, qi_ref, w_ref, *rest, pgs):
    pages, key_ref = rest[:pgs], rest[pgs]
    qi32, w32 = qi_ref[...], w_ref[...]
    for p in range(pgs):
        key_ref[:, p * PAGE_SIZE:(p + 1) * PAGE_SIZE] = _ordered_key(
            _idx_score(qi32, w32, pages[p][...].astype(BF16)))


def _dsa_s_scores(page_table, qi32, w32, kidx_t):
    nb, n_pages = page_table.shape
    pgs = math.gcd(n_pages, 16)
    nj = n_pages // pgs

    def page_spec(p):
        return pl.BlockSpec((None, IDX_DIM, PAGE_SIZE), lambda b, j, pt: (pt[b, j * pgs + p], 0, 0))

    grid_spec = pltpu.PrefetchScalarGridSpec(
        num_scalar_prefetch=1, grid=(nb, nj),
        in_specs=[pl.BlockSpec((None, IDX_HEADS * QR, IDX_DIM), lambda b, j, pt: (b, 0, 0)),
                  pl.BlockSpec((None, IDX_HEADS * QR, 128), lambda b, j, pt: (b, 0, 0))]
                 + [page_spec(p) for p in range(pgs)],
        out_specs=pl.BlockSpec((None, QR, pgs * PAGE_SIZE), lambda b, j, pt: (b, 0, j)))
    return pl.pallas_call(
        functools.partial(_dsa_s_scores_body, pgs=pgs), grid_spec=grid_spec,
        out_shape=jax.ShapeDtypeStruct((nb, QR, n_pages * PAGE_SIZE), I32),
        compiler_params=_cparams(("parallel", "parallel")), name="dsa_sample_scores",
    )(page_table, qi32, w32, *([kidx_t] * pgs))


SQ = 8


def _dsa_s_select_body(key_ref, qi_ref, w_ref, kin_ref, mb_ref, *, n_pages, ksel, t_new):
    R = SQ * QR
    row = lax.broadcasted_iota(I32, (QR, PAGE_SIZE), 0)
    col = lax.broadcasted_iota(I32, (QR, PAGE_SIZE), 1)
    new_ok1 = (col <= row) & (col < t_new)
    key_new = jnp.concatenate(
        [_ordered_key(jnp.where(new_ok1, _idx_score(qi_ref[q], w_ref[q], kin_ref[q]), -jnp.inf)) for q in range(SQ)],
        axis=0)
    new_ok = jnp.concatenate([new_ok1] * SQ, axis=0)

    def slab(s):
        if s == n_pages:
            return key_new
        return key_ref[:, :, s * PAGE_SIZE:(s + 1) * PAGE_SIZE].reshape(R, PAGE_SIZE)

    def count(pred_fn):
        cnt = jnp.zeros((R, 128), F32)
        for s in range(n_pages + 1):
            cnt = cnt + jnp.where(pred_fn(slab(s)), 1.0, 0.0)
        return jnp.sum(cnt, axis=1, keepdims=True)

    def bit_step(it, thr):
        cand = thr + jnp.left_shift(jnp.int32(1), 31 - it)
        tot = count(lambda x: x >= cand)
        return jnp.where(_bcast_lanes(tot, 128) >= ksel, cand, thr)

    thr = lax.fori_loop(0, 32, bit_step, jnp.full((R, 128), INT_MIN, I32))
    need = _bcast_lanes(ksel - count(lambda x: x > thr), 128)
    triu = jnp.where(lax.broadcasted_iota(I32, (128, 128), 0) < lax.broadcasted_iota(I32, (128, 128), 1),
                     1.0, 0.0).astype(BF16)
    ones = jnp.ones((128, 128), BF16)
    carry = jnp.zeros((R, 128), F32)
    for s in range(n_pages + 1):
        x = slab(s)
        eq = x == thr
        eqb = jnp.where(eq, 1.0, 0.0).astype(BF16)
        rank = jnp.dot(eqb, triu, preferred_element_type=F32) + carry
        mb = jnp.where(x > thr, 0.0, jnp.where(eq, jnp.where(rank < need, 0.0, NEG), NEG))
        if s == n_pages:
            mb = jnp.where(new_ok, mb, NEG)
        mb_ref[:, :, s * PAGE_SIZE:(s + 1) * PAGE_SIZE] = mb.reshape(SQ, QR, PAGE_SIZE)
        carry = carry + jnp.dot(eqb, ones, preferred_element_type=F32)


def _dsa_s_select(keys, qi32, w32, ki_new_t, ksel, t_new):
    nb, _, L = keys.shape
    n_pages = L // PAGE_SIZE
    Lp = L + PAGE_SIZE
    assert nb % SQ == 0
    return pl.pallas_call(
        functools.partial(_dsa_s_select_body, n_pages=n_pages, ksel=ksel, t_new=t_new),
        grid=(nb // SQ,),
        in_specs=[pl.BlockSpec((SQ, QR, L), lambda i: (i, 0, 0)),
                  pl.BlockSpec((SQ, IDX_HEADS * QR, IDX_DIM), lambda i: (i, 0, 0)),
                  pl.BlockSpec((SQ, IDX_HEADS * QR, 128), lambda i: (i, 0, 0)),
                  pl.BlockSpec((SQ, IDX_DIM, PAGE_SIZE), lambda i: (i, 0, 0))],
        out_specs=pl.BlockSpec((SQ, QR, Lp), lambda i: (i, 0, 0)),
        out_shape=jax.ShapeDtypeStruct((nb, QR, Lp), F32),
        compiler_params=_cparams(("parallel",)), name="dsa_sample_select",
    )(keys, qi32, w32, ki_new_t)


def _dsa_s_attn_body(pt_ref, q_ref, mb_ref, mbn_ref, kvn_ref, bias_ref, *rest):
    kpages = rest[:PG]
    vpages = rest[PG:2 * PG]
    o_ref = rest[2 * PG]
    m_scr, l_scr, acc_scr = rest[2 * PG + 1:]
    j = pl.program_id(1)
    last = j == pl.num_programs(1) - 1
    scale = (B_HEAD_DIM ** -0.5) * LOG2E

    @pl.when(j == 0)
    def _init():
        m_scr[...] = jnp.full(m_scr.shape, NEG, F32)
        l_scr[...] = jnp.zeros(l_scr.shape, F32)
        acc_scr[...] = jnp.zeros(acc_scr.shape, F32)

    R = B_GROUP * QR
    nrep = B_KV_HEADS * B_GROUP
    qs = [(q_ref[g] * scale).astype(BF16) for g in range(B_KV_HEADS)]

    def near_bias(kind):
        return jnp.concatenate([bias_ref[h, kind] for h in range(B_HEADS)], axis=0)

    def update(s, vs):
        m_old = m_scr[...]
        m_new = jnp.maximum(m_old, jnp.max(s, axis=1, keepdims=True))
        alpha = jnp.exp2(m_old - m_new)
        p = jnp.exp2(s - m_new)
        l_scr[...] = alpha * l_scr[...] + jnp.sum(p, axis=1, keepdims=True)
        pb = p.astype(BF16)
        pv = jnp.concatenate([jnp.dot(pb[g * R:(g + 1) * R], vs[g], preferred_element_type=F32)
                              for g in range(B_KV_HEADS)], axis=0)
        acc_scr[...] = alpha * acc_scr[...] + pv
        m_scr[...] = m_new

    ks = [jnp.concatenate([kpages[p][pl.ds(g, PAGE_SIZE, stride=B_KV_HEADS), :].astype(BF16)
                           for p in range(PG)], axis=0) for g in range(B_KV_HEADS)]
    vs = [jnp.concatenate([vpages[p][pl.ds(g, PAGE_SIZE, stride=B_KV_HEADS), :].astype(BF16)
                           for p in range(PG)], axis=0) for g in range(B_KV_HEADS)]
    s = jnp.concatenate([lax.dot_general(qs[g], ks[g], NT_DIMS, preferred_element_type=F32)
                         for g in range(B_KV_HEADS)], axis=0)
    s = s + jnp.concatenate([mb_ref[...]] * nrep, axis=0)
    nb0 = jnp.where(last, near_bias(0), 0.0)
    s = jnp.concatenate([s[:, :(PG - 1) * PAGE_SIZE], s[:, (PG - 1) * PAGE_SIZE:] + nb0], axis=1)
    update(s, vs)

    @pl.when(last)
    def _fin():
        CW = B_KV_HEADS * B_HEAD_DIM
        kn = [kvn_ref[:, g * B_HEAD_DIM:(g + 1) * B_HEAD_DIM].astype(BF16) for g in range(B_KV_HEADS)]
        vn = [kvn_ref[:, CW + g * B_HEAD_DIM:CW + (g + 1) * B_HEAD_DIM].astype(BF16) for g in range(B_KV_HEADS)]
        sn = jnp.concatenate([lax.dot_general(qs[g], kn[g], NT_DIMS, preferred_element_type=F32)
                              for g in range(B_KV_HEADS)], axis=0)
        sn = sn + jnp.concatenate([mbn_ref[...]] * nrep, axis=0) + near_bias(1)
        update(sn, vn)
        out = acc_scr[...] * (1.0 / l_scr[...])
        for g in range(B_KV_HEADS):
            o_ref[g] = out[g * R:(g + 1) * R]


def _dsa_s_attn(page_table, q32, mb, kv_new, bias_s, ck_rows, cv_rows):
    nb, n_pages = page_table.shape
    assert n_pages % PG == 0
    nj = n_pages // PG
    R = B_GROUP * QR
    PR = PAGE_SIZE * B_KV_HEADS

    def page_spec(p):
        return pl.BlockSpec((PR, B_HEAD_DIM), lambda b, j, pt: (pt[b, j * PG + p], 0))

    grid_spec = pltpu.PrefetchScalarGridSpec(
        num_scalar_prefetch=1, grid=(nb, nj),
        in_specs=[pl.BlockSpec((None, B_KV_HEADS, R, B_HEAD_DIM), lambda b, j, pt: (b, 0, 0, 0)),
                  pl.BlockSpec((None, QR, PG * PAGE_SIZE), lambda b, j, pt: (b, 0, j)),
                  pl.BlockSpec((None, QR, PAGE_SIZE), lambda b, j, pt: (b, 0, n_pages)),
                  pl.BlockSpec((None, PAGE_SIZE, 2 * B_KV_HEADS * B_HEAD_DIM), lambda b, j, pt: (b, 0, 0)),
                  pl.BlockSpec(bias_s.shape, lambda b, j, pt: (0, 0, 0, 0))]
                 + [page_spec(p) for p in range(PG)] + [page_spec(p) for p in range(PG)],
        out_specs=pl.BlockSpec((None, B_KV_HEADS, R, B_HEAD_DIM), lambda b, j, pt: (b, 0, 0, 0)),
        scratch_shapes=[pltpu.VMEM((B_KV_HEADS * R, 1), F32), pltpu.VMEM((B_KV_HEADS * R, 1), F32),
                        pltpu.VMEM((B_KV_HEADS * R, B_HEAD_DIM), F32)])
    return pl.pallas_call(
        _dsa_s_attn_body, grid_spec=grid_spec,
        out_shape=jax.ShapeDtypeStruct((nb, B_KV_HEADS, R, B_HEAD_DIM), F32),
        compiler_params=_cparams(("parallel", "arbitrary")), name="dsa_sample_attn",
    )(page_table, q32, mb, mb, kv_new, bias_s, *([ck_rows] * PG), *([cv_rows] * PG))


def _layer_norm(r, g, b):
    mu = jnp.mean(r, axis=-1, keepdims=True)
    d = r - mu
    var = jnp.mean(d * d, axis=-1, keepdims=True)
    return d * lax.rsqrt(var + LN_EPS) * g + b


def _merge_body(oa_ref, ob_ref, gates_ref, x_ref, g1_ref, sh2_ref, sc2_ref, w_ref, lng_ref, lnb_ref,
                x1_ref, h2_ref, *, alpha):
    D = oa_ref.shape[1]
    merged = _sigmoid(gates_ref[:, 0:D]) * oa_ref[...] + _sigmoid(gates_ref[:, D:2 * D]) * ob_ref[...]
    y = jnp.dot(merged.astype(BF16), w_ref[...], preferred_element_type=F32)
    x1 = _layer_norm(alpha * x_ref[...] + g1_ref[...] * y, lng_ref[...], lnb_ref[...])
    x1_ref[...] = x1
    h2_ref[...] = (x1 * (1.0 + sc2_ref[...]) + sh2_ref[...]).astype(BF16)


def _merge(oa, ob, z, x2d, mod, rows_per_mod, w_out_bf, ln_g, ln_b, alpha, tm):
    N, D = x2d.shape
    row = lambda i: (i, 0)
    const = lambda i: (0, 0)
    return pl.pallas_call(
        functools.partial(_merge_body, alpha=alpha), grid=(N // tm,),
        in_specs=[pl.BlockSpec((tm, D), row), pl.BlockSpec((tm, D), row),
                  pl.BlockSpec((tm, 2 * D), lambda i: (i, Z_GATES // (2 * D))),
                  pl.BlockSpec((tm, D), row),
                  _mod_spec(mod, 2, tm, rows_per_mod, D, 1),
                  _mod_spec(mod, 3, tm, rows_per_mod, D, 1),
                  _mod_spec(mod, 4, tm, rows_per_mod, D, 1),
                  pl.BlockSpec((D, D), const), pl.BlockSpec((1, D), const), pl.BlockSpec((1, D), const)],
        out_specs=[pl.BlockSpec((tm, D), row), pl.BlockSpec((tm, D), row)],
        out_shape=[jax.ShapeDtypeStruct((N, D), F32), jax.ShapeDtypeStruct((N, D), BF16)],
        compiler_params=_cparams(("parallel",)), name="merge_out",
    )(oa, ob, z, x2d, mod, mod, mod, w_out_bf, ln_g, ln_b)


def _top_values(x, n, with_rank=False):
    out = []
    rank = jnp.full(x.shape, float(n), F32) if with_rank else None
    for r in range(n):
        m = jnp.max(x, axis=0, keepdims=True)
        out.append(m)
        if with_rank:
            rank = jnp.where(x == m, float(r), rank)
        if r < n - 1:
            x = jnp.where(x == m, -jnp.inf, x)
    return (out, rank) if with_rank else out


_PEER_PAIRS = [(a, b) for a in range(PEER_TOPK) for b in range(PEER_TOPK) if (a + 1) * (b + 1) <= PEER_TOPK]


def _peer_body(h2_ref, x1_ref, g2_ref, wq_ref, sk_ref, u0_ref, un_ref, vt_ref, lng_ref, lnb_ref, o_ref,
               np_scr, r2_scr, e1_scr, e2_scr, acc_scr, act_a, act_b, w_ab, *, tn, ec, nchunks, alpha):
    e = pl.program_id(1)
    ne = pl.num_programs(1) - 1
    n_i1 = ec // N_KEYS

    @pl.when(e == 0)
    def _select():
        q = jnp.dot(h2_ref[...], wq_ref[...], preferred_element_type=F32)
        for h in range(PEER_HEADS):
            sc = []
            for c in range(2):
                qhc = q[:, (2 * h + c) * N_KEYS:(2 * h + c + 1) * N_KEYS].astype(BF16)
                sc.append(lax.dot_general(sk_ref[c], qhc, NT_DIMS, preferred_element_type=F32))
            s1, s2 = sc
            t1 = _top_values(s1, PEER_TOPK)
            t2, rank2 = _top_values(s2, PEER_TOPK, with_rank=True)
            cand = jnp.concatenate([t1[a] + t2[b] for a, b in _PEER_PAIRS], axis=0)
            thr = _top_values(cand, PEER_TOPK)[-1]
            mx = t1[0] + t2[0]
            zsum = jnp.sum(jnp.where(cand >= thr, jnp.exp(cand - mx), 0.0), axis=0, keepdims=True)
            passed = jnp.where(cand >= thr, 1.0, 0.0)
            npass = jnp.zeros(s1.shape, F32)
            for a in range(PEER_TOPK):
                rows = [i for i, (pa, _) in enumerate(_PEER_PAIRS) if pa == a]
                np_a = passed[rows[0]:rows[0] + 1]
                for i in rows[1:]:
                    np_a = np_a + passed[i:i + 1]
                npass = jnp.where(s1 == t1[a], np_a, npass)
            np_scr[h] = npass
            r2_scr[h] = rank2.astype(BF16)
            e1_scr[h] = jnp.exp(s1 - t1[0]) * (1.0 / zsum)
            e2_scr[h] = jnp.exp(s2 - t2[0]).astype(BF16)
        act_a[...] = lax.dot_general(u0_ref[...], h2_ref[...], NT_DIMS, preferred_element_type=F32)
        w_ab[1] = jnp.zeros(w_ab.shape[1:], BF16)
        acc_scr[...] = jnp.zeros(acc_scr.shape, F32)

    PK = 16

    def chunk(act_cur, act_nxt, w_cur, w_prev):
        acc_scr[...] += jnp.dot(vt_ref[...], w_prev[...], preferred_element_type=F32)

        act_nxt[...] = lax.dot_general(un_ref[...], h2_ref[...], NT_DIMS, preferred_element_type=F32)
        for jj in range(n_i1):
            i1 = e * n_i1 + jj
            a = act_cur[jj * N_KEYS:(jj + 1) * N_KEYS, :]
            gsum = jnp.zeros((N_KEYS // PK, PK, tn), BF16)
            for h in range(PEER_HEADS):
                npass_row = jnp.broadcast_to(np_scr[h, pl.ds(i1, 1), :], (PK, tn)).astype(BF16)
                e1_row = jnp.broadcast_to(e1_scr[h, pl.ds(i1, 1), :], (PK, tn)).astype(BF16)
                sel = r2_scr[h].reshape(N_KEYS // PK, PK, tn) < npass_row
                e2 = e2_scr[h].reshape(N_KEYS // PK, PK, tn)
                gsum = gsum + jnp.where(sel, e2, jnp.zeros_like(e2)) * e1_row
            gelu = 0.5 * a * (1.0 + lax.erf(a * (2.0 ** -0.5)))
            w_cur[jj * N_KEYS:(jj + 1) * N_KEYS, :] = gsum.reshape(N_KEYS, tn) * gelu.astype(BF16)

    @pl.when((e < ne) & (e % 2 == 0))
    def _even():
        chunk(act_a, act_b, w_ab.at[e % 2], w_ab.at[1 - e % 2])

    @pl.when((e < ne) & (e % 2 == 1))
    def _odd():
        chunk(act_b, act_a, w_ab.at[e % 2], w_ab.at[1 - e % 2])

    @pl.when(e == ne)
    def _fin():
        acc = acc_scr[...] + jnp.dot(vt_ref[...], w_ab[(nchunks - 1) % 2], preferred_element_type=F32)
        y = acc.T
        o_ref[...] = _layer_norm(alpha * x1_ref[...] + g2_ref[...] * y, lng_ref[...], lnb_ref[...])


def _peer(h2, x1, mod, rows_per_mod, wq_bf, sk_bf, u_bf, vt_bf, ln_g, ln_b, alpha, tn, ec):
    N, D = x1.shape
    E = u_bf.shape[0]
    ne = E // ec
    row = lambda i, e: (i, 0)
    const2 = lambda i, e: (0, 0)
    if mod.ndim == 3:
        g2_spec = pl.BlockSpec((None, 1, D), lambda i, e: ((i * tn) // rows_per_mod, 0, 5))
    else:
        g2_spec = pl.BlockSpec((tn, D), lambda i, e: (i, 5))
    return pl.pallas_call(
        functools.partial(_peer_body, tn=tn, ec=ec, nchunks=ne, alpha=alpha), grid=(N // tn, ne + 1),
        in_specs=[pl.BlockSpec((tn, D), row), pl.BlockSpec((tn, D), row), g2_spec,
                  _resident(wq_bf.shape, const2),
                  _resident(sk_bf.shape, lambda i, e: (0, 0, 0)),
                  _resident((ec, D), const2),
                  pl.BlockSpec((ec, D), lambda i, e: (jnp.minimum(e + 1, ne - 1), 0)),
                  pl.BlockSpec((D, ec), lambda i, e: (0, jnp.maximum(e - 1, 0))),
                  pl.BlockSpec((1, D), const2), pl.BlockSpec((1, D), const2)],
        out_specs=pl.BlockSpec((tn, D), row),
        out_shape=jax.ShapeDtypeStruct((N, D), F32),
        scratch_shapes=[pltpu.VMEM((PEER_HEADS, N_KEYS, tn), F32), pltpu.VMEM((PEER_HEADS, N_KEYS, tn), BF16),
                        pltpu.VMEM((PEER_HEADS, N_KEYS, tn), F32), pltpu.VMEM((PEER_HEADS, N_KEYS, tn), BF16),
                        pltpu.VMEM((D, tn), F32),
                        pltpu.VMEM((ec, tn), F32), pltpu.VMEM((ec, tn), F32), pltpu.VMEM((2, ec, tn), BF16)],
        compiler_params=_cparams(("parallel", "arbitrary")), name="peer",
    )(h2, x1, mod, wq_bf, sk_bf, u_bf, u_bf, vt_bf, ln_g, ln_b)


def _pack_w_in(w_in):
    D = w_in.shape[0]
    idx_w = w_in[:, 5632:5956]
    pad = jnp.zeros((D, Z_GATES - Z_QI - idx_w.shape[1]), w_in.dtype)
    return jnp.concatenate([w_in[:, :5632], idx_w, pad, w_in[:, 5956:]], axis=1).astype(BF16)


def _prompt_buckets(tq):
    i = np.arange(tq)[:, None]
    j = np.arange(tq)[None, :]
    return np.stack([_t5_bucket_np(np.maximum(i - j, 0)), _t5_bucket_np(tq + i - j)])


def _sample_buckets():
    t = np.arange(QR)[:, None]
    o = np.arange(PAGE_SIZE)[None, :]
    return np.stack([_t5_bucket_np(PAGE_SIZE + t - o), _t5_bucket_np(np.maximum(t - o, 0))])


def _layer_tail(oa, ob, z, x2d, mod, rows_per_mod, wts, alpha, tm, tn_peer):
    x1, h2 = _merge(oa, ob, z, x2d, mod, rows_per_mod, wts["w_out"], wts["ln1_g"], wts["ln1_b"], alpha, tm)
    return _peer(h2, x1, mod, rows_per_mod, wts["w_peer_q"], wts["sub_keys"], wts["peer_u"], wts["peer_vt"],
                 wts["ln2_g"], wts["ln2_b"], alpha, tn_peer, 1024)


def kernel(x_prompt, x_sample, cache_k, cache_v, cache_kidx, state_hgrn, page_table, c_prompt, c_sample,
           w_ada, b_ada, w_in, lb_param, a_gnorm, t5_table, w_out, ln1_g, ln1_b, w_peer_q, peer_sub_keys,
           peer_u, peer_v, ln2_g, ln2_b):
    depth = w_ada.shape[0]
    assert depth == 1 and lb_param.shape[0] == 2
    alpha = (2.0 * depth) ** 0.25
    B, T, D = x_prompt.shape
    NB, TS, _ = x_sample.shape
    n_pages = page_table.shape[1]
    past_len = n_pages * PAGE_SIZE

    wts = dict(
        w_out=w_out[0].astype(BF16), ln1_g=ln1_g, ln1_b=ln1_b, ln2_g=ln2_g, ln2_b=ln2_b,
        w_peer_q=w_peer_q[0].astype(BF16), sub_keys=peer_sub_keys[0].astype(BF16),
        peer_u=peer_u[0].astype(BF16), peer_vt=peer_v[0].T.astype(BF16))
    w_in_bf = _pack_w_in(w_in[0])
    gnorm = a_gnorm.reshape(1, A_DK)

    mod = _ada(jnp.concatenate([c_prompt, c_sample], axis=0), w_ada[0].astype(BF16), b_ada)
    mod_p = mod[:B].reshape(B, 1, 6 * D)
    mod_s = jnp.repeat(mod[B:], TS, axis=0)

    xp = x_prompt.reshape(B * T, D)
    zp, kp_rows, vp_rows = _inproj(xp, mod_p, T, w_in_bf, 256, 1024)
    oa_p, s_p = _hgrn(zp, jnp.zeros((B, A_HEADS, A_DK, A_DK), F32), lb_param, gnorm, B, T, 256, A_CHUNK, 256, 8)
    tq = 256
    bias_p = _bias_tiles(t5_table, _prompt_buckets(tq))
    bias_g = bias_p.reshape(B_KV_HEADS, B_GROUP, 2, tq, tq).transpose(0, 2, 1, 3, 4)
    bias_g = bias_g.reshape(B_KV_HEADS, 2, B_GROUP * tq, tq)
    bias_g = jnp.concatenate([bias_g, jnp.zeros((B_KV_HEADS, 1, B_GROUP * tq, tq), F32)], axis=1)
    ob_p = _dsa_prompt(zp, bias_g, B, T, tq)
    y_p = _layer_tail(oa_p, ob_p, zp, xp, mod_p, T, wts, alpha, 256, 512)

    xs = x_sample.reshape(NB * TS, D)
    zs, ks_rows, vs_rows = _inproj(xs, mod_s, 1, w_in_bf, min(256, NB * TS), 1024)
    zs3 = zs.reshape(NB, TS, Z_WIDTH)
    zs_pad = jnp.pad(zs3, ((0, 0), (0, QR - TS), (0, 0))).reshape(NB * QR, Z_WIDTH)
    assert TS <= QR
    oa_s8, s_s = _hgrn(zs_pad, state_hgrn[0], lb_param, gnorm, NB, QR, QR, QR, TS, A_HEADS)
    oa_s = oa_s8.reshape(NB, QR, D)[:, :TS].reshape(NB * TS, D)

    qi_s = zs3[:, :, Z_QI:Z_QI + IDX_HEADS * IDX_DIM].reshape(NB, TS, IDX_HEADS, IDX_DIM)
    qi32 = jnp.pad(qi_s.transpose(0, 2, 1, 3), ((0, 0), (0, 0), (0, QR - TS), (0, 0)))
    qi32 = qi32.reshape(NB, IDX_HEADS * QR, IDX_DIM).astype(BF16)
    wi_s = zs3[:, :, Z_WI:Z_WI + IDX_HEADS].transpose(0, 2, 1)
    w32 = jnp.pad(wi_s, ((0, 0), (0, 0), (0, QR - TS))).reshape(NB, IDX_HEADS * QR, 1)
    w32 = jnp.broadcast_to(w32, (NB, IDX_HEADS * QR, 128))
    ki_new = jnp.pad(zs3[:, :, Z_KI:Z_KI + IDX_DIM], ((0, 0), (0, PAGE_SIZE - TS), (0, 0))).astype(BF16)
    ki_new_t = jnp.swapaxes(ki_new, 1, 2)
    kv_new = jnp.pad(zs3[:, :, Z_KB:Z_KB + 512], ((0, 0), (0, PAGE_SIZE - TS), (0, 0)))
    q_s = zs3[:, :, Z_QB:Z_QB + 1024].reshape(NB, TS, B_KV_HEADS, B_GROUP, B_HEAD_DIM)
    q32 = jnp.pad(q_s.transpose(0, 2, 3, 1, 4), ((0, 0), (0, 0), (0, 0), (0, QR - TS), (0, 0)))
    q32 = q32.reshape(NB, B_KV_HEADS, B_GROUP * QR, B_HEAD_DIM)

    ksel = min(TOPK_MAX, (past_len + TS) // 4)
    kidx_t = jnp.swapaxes(cache_kidx[0], 1, 2)
    keys_s = _dsa_s_scores(page_table, qi32, w32, kidx_t)
    mb_s = _dsa_s_select(keys_s, qi32, w32, ki_new_t, ksel, TS)
    bias_s = _bias_tiles(t5_table, _sample_buckets())
    n_pool = cache_k.shape[1]
    ck = cache_k.reshape(n_pool * PAGE_SIZE * B_KV_HEADS, B_HEAD_DIM)
    cv = cache_v.reshape(n_pool * PAGE_SIZE * B_KV_HEADS, B_HEAD_DIM)
    o32 = _dsa_s_attn(page_table, q32, mb_s, kv_new, bias_s, ck, cv)
    ob_s = o32.reshape(NB, B_KV_HEADS, B_GROUP, QR, B_HEAD_DIM)[:, :, :, :TS]
    ob_s = ob_s.transpose(0, 3, 1, 2, 4).reshape(NB * TS, D)
    y_s = _layer_tail(oa_s, ob_s, zs, xs, mod_s, 1, wts, alpha, NB * TS, NB * TS)

    def kv_out(z, k_rows, v_rows, nb, t):
        k = k_rows.reshape(1, nb, t, B_KV_HEADS, B_HEAD_DIM)
        v = v_rows.reshape(1, nb, t, B_KV_HEADS, B_HEAD_DIM)
        ki = z[:, Z_KI:Z_KI + IDX_DIM].reshape(1, nb, t, IDX_DIM)
        return k, v, ki

    kp, vp, ip = kv_out(zp, kp_rows, vp_rows, B, T)
    ks, vs, is_ = kv_out(zs, ks_rows, vs_rows, NB, TS)
    return (y_p.reshape(B, T, D), y_s.reshape(NB, TS, D), kp, vp, ip, s_p[None],
            ks, vs, is_, s_s[None])
```
